```python
import jax, jax.numpy as jnp
from jax import lax
import numpy as np

D_MODEL = 2048
BATCH = 2
SEQ = 4096
DEPTH = 4
DEC_BATCH = 16
DEC_SEQ = 16
PAST_LEN = 4096

CHUNK = 64
Q_BLOCK = 2 * CHUNK
N_META = 16
HEAD_DIM = 64
D_SB = D_MODEL // 2
D_RW = D_MODEL - D_SB
D_MIX = D_SB + D_RW
H_SB = D_SB // HEAD_DIM
H_RW = D_RW // HEAD_DIM
R_DECAY = 64
R_ICL = 64
R_GATE = 160
D_FF = 5632
P_SB = 3 * D_SB
P_RW = 3 * D_RW + R_DECAY + R_ICL + R_GATE
P_IN = P_SB + P_RW
NORM_EPS = 1e-6
GN_EPS = 64e-5

kernel_name = "hybrid_stickbreak_rwkv7_macaron_stream_step"


def _rmsnorm(x, g, eps=NORM_EPS):
    xf = x.astype(jnp.float32)
    y = xf * lax.rsqrt(jnp.mean(xf * xf, axis=-1, keepdims=True) + eps)
    return (y * g.astype(jnp.float32)).astype(x.dtype)


def _half_ffn(h, g_pre, g_post, w_gate, w_up, w_down):
    u = _rmsnorm(h, g_pre)
    f = (jax.nn.silu(u @ w_gate) * (u @ w_up)) @ w_down
    return h + 0.5 * _rmsnorm(f, g_post)


def _stick_breaking_block(q, k, v, q_pos0):
    tq, tk = q.shape[1], k.shape[1]
    z = jnp.einsum("bqhe,bkhe->bhqk", q, k).astype(jnp.float32) * (HEAD_DIM ** -0.5)
    q_pos = q_pos0 + jnp.arange(tq)
    k_pos = jnp.arange(tk)
    mask = k_pos[None, :] < q_pos[:, None]
    log_stay = jnp.where(mask, jax.nn.log_sigmoid(-z), 0.0)
    between = lax.cumsum(log_stay, axis=3, reverse=True) - log_stay
    a = jnp.where(mask, jnp.exp(jax.nn.log_sigmoid(z) + between), 0.0)
    return jnp.einsum("bhqk,bkhe->bqhe", a, v.astype(jnp.float32))


def _stick_breaking(q, k_all, v_all, q_offset):
    tq = q.shape[1]
    outs = []
    for i0 in range(0, tq, Q_BLOCK):
        i1 = min(i0 + Q_BLOCK, tq)
        n_keys = q_offset + i1
        outs.append(_stick_breaking_block(q[:, i0:i1], k_all[:, :n_keys], v_all[:, :n_keys], q_offset + i0))
    return jnp.concatenate(outs, axis=1)


def _rwkv7_scan(S0, r, w, k, v, a, b):
    def step(S, inp):
        r_t, w_t, k_t, v_t, a_t, b_t = inp
        sa = jnp.einsum("bhij,bhj->bhi", S, a_t)
        S = S * w_t[:, :, None, :] + sa[..., None] * b_t[:, :, None, :] + v_t[..., None] * k_t[:, :, None, :]
        return S, jnp.einsum("bhij,bhj->bhi", S, r_t)
    xs = tuple(jnp.swapaxes(t, 0, 1) for t in (r, w, k, v, a, b))
    S_T, y = lax.scan(step, S0, xs)
    return S_T, jnp.swapaxes(y, 0, 1)


def _rwkv7_mix(p, shift0, S0, mu, w0, w2, a0, a2, g2, k_k, k_a, r_k, ln_w, ln_b):
    f32 = jnp.float32
    bsz, t, _ = p.shape
    prev = jnp.concatenate([shift0[:, None, :].astype(p.dtype), p[:, :-1]], axis=1)
    xs = p + (prev - p) * mu
    o1, o2, o3 = D_RW, 2 * D_RW, 3 * D_RW
    o4, o5 = o3 + R_DECAY, o3 + R_DECAY + R_ICL
    r, k, v = xs[..., :o1], xs[..., o1:o2], xs[..., o2:o3]
    dw, da, dg = xs[..., o3:o4], xs[..., o4:o5], xs[..., o5:]
    w_log = -jax.nn.softplus(-(w0 + jnp.tanh(dw) @ w2).astype(f32)) - 0.5
    decay = jnp.exp(-jnp.exp(w_log))
    a = jax.nn.sigmoid((a0 + da @ a2).astype(f32))
    g = (jax.nn.sigmoid(dg) @ g2).astype(f32)
    heads = lambda x: x.reshape(bsz, t, H_RW, HEAD_DIM)
    kf = k.astype(f32)
    kk = heads(kf * k_k)
    kk = kk / jnp.maximum(jnp.sqrt(jnp.sum(kk * kk, axis=-1, keepdims=True)), 1e-12)
    k_mod = kf * (1.0 + (a - 1.0) * k_a)
    rh, kh, vh, ah = heads(r.astype(f32)), heads(k_mod), heads(v.astype(f32)), heads(a)
    S_T, y = _rwkv7_scan(S0.astype(f32), rh, heads(decay), kh, vh, -kk, kk * ah)
    mean = jnp.mean(y, axis=-1, keepdims=True)
    var = jnp.mean(jnp.square(y - mean), axis=-1, keepdims=True)
    y = ((y - mean) * lax.rsqrt(var + GN_EPS)).reshape(bsz, t, D_RW) * ln_w + ln_b
    bonus = jnp.sum(rh * kh * r_k, axis=-1, keepdims=True) * vh
    y = (y + bonus.reshape(bsz, t, D_RW)) * g
    return y.astype(p.dtype), S_T.astype(S0.dtype), p[:, -1].astype(shift0.dtype)


def _layer(h, k_past, v_past, S0, shift0, lw):
    bsz, t, _ = h.shape
    h = _half_ffn(h, lw["ffn1_norm_pre"], lw["ffn1_norm_post"], lw["ffn1_w_gate"], lw["ffn1_w_up"], lw["ffn1_w_down"])
    u = _rmsnorm(h, lw["mix_norm_pre"])
    p = u @ lw["w_in"]
    q = p[..., :D_SB].reshape(bsz, t, H_SB, HEAD_DIM)
    k = p[..., D_SB:2 * D_SB].reshape(bsz, t, H_SB, HEAD_DIM)
    v = p[..., 2 * D_SB:P_SB].reshape(bsz, t, H_SB, HEAD_DIM)
    if k_past is None:
        k_all, v_all, past = k, v, 0
    else:
        k_all = jnp.concatenate([k_past.astype(k.dtype), k], axis=1)
        v_all = jnp.concatenate([v_past.astype(v.dtype), v], axis=1)
        past = k_past.shape[1]
    o_sb = _stick_breaking(q, k_all, v_all, past)
    o_sb = _rmsnorm(o_sb, lw["sb_out_gain"]).reshape(bsz, t, D_SB).astype(h.dtype)
    o_rw, S_T, shift_T = _rwkv7_mix(p[..., P_SB:], shift0, S0, lw["rwkv_mu"], lw["rwkv_w0"], lw["rwkv_w2"],
                                    lw["rwkv_a0"], lw["rwkv_a2"], lw["rwkv_g2"], lw["rwkv_k_k"], lw["rwkv_k_a"],
                                    lw["rwkv_r_k"], lw["rwkv_ln_w"], lw["rwkv_ln_b"])
    o = jnp.concatenate([o_sb, o_rw], axis=-1) @ lw["w_out"]
    h = h + _rmsnorm(o, lw["mix_norm_post"])
    h = _half_ffn(h, lw["ffn2_norm_pre"], lw["ffn2_norm_post"], lw["ffn2_w_gate"], lw["ffn2_w_up"], lw["ffn2_w_down"])
    return h, k, v, S_T, shift_T


def setup_inputs(seed: int = 0) -> dict:
    key = jax.random.key(seed)
    ks = iter(jax.random.split(key, 40))
    nrm = lambda shape, s=1.0: jax.random.normal(next(ks), shape, jnp.float32) * s
    gain = lambda shape: 1.0 + 0.02 * jax.random.normal(next(ks), shape, jnp.float32)
    L, D = DEPTH, D_MODEL
    return {
        "x_prompt": nrm((BATCH, SEQ, D)),
        "x_sample": nrm((DEC_BATCH, DEC_SEQ, D)),
        "cache_sb_k": nrm((L, DEC_BATCH, PAST_LEN, H_SB, HEAD_DIM)),
        "cache_sb_v": nrm((L, DEC_BATCH, PAST_LEN, H_SB, HEAD_DIM)),
        "state_rwkv_S": nrm((L, DEC_BATCH, H_RW, HEAD_DIM, HEAD_DIM)),
        "state_rwkv_shift": nrm((L, DEC_BATCH, P_RW)),
        "meta_tokens": nrm((N_META, D)),
        "ffn1_norm_pre": gain((L, D)),
        "ffn1_norm_post": gain((L, D)),
        "ffn1_w_gate": nrm((L, D, D_FF), D ** -0.5),
        "ffn1_w_up": nrm((L, D, D_FF), D ** -0.5),
        "ffn1_w_down": nrm((L, D_FF, D), D_FF ** -0.5),
        "mix_norm_pre": gain((L, D)),
        "mix_norm_post": gain((L, D)),
        "w_in": nrm((L, D, P_IN), D ** -0.5),
        "sb_out_gain": gain((L, H_SB, HEAD_DIM)),
        "rwkv_mu": jax.random.uniform(next(ks), (L, P_RW), jnp.float32),
        "rwkv_w0": jax.random.uniform(next(ks), (L, D_RW), jnp.float32, -6.0, 1.0),
        "rwkv_w2": nrm((L, R_DECAY, D_RW), 0.5 * R_DECAY ** -0.5),
        "rwkv_a0": nrm((L, D_RW), 0.5),
        "rwkv_a2": nrm((L, R_ICL, D_RW), 0.5 * R_ICL ** -0.5),
        "rwkv_g2": nrm((L, R_GATE, D_RW), R_GATE ** -0.5),
        "rwkv_k_k": 0.85 + 0.02 * nrm((L, D_RW)),
        "rwkv_k_a": gain((L, D_RW)),
        "rwkv_r_k": nrm((L, H_RW, HEAD_DIM), 0.1),
        "rwkv_ln_w": gain((L, D_RW)),
        "rwkv_ln_b": nrm((L, D_RW), 0.01),
        "w_out": nrm((L, D_MIX, D), D_MIX ** -0.5),
        "ffn2_norm_pre": gain((L, D)),
        "ffn2_norm_post": gain((L, D)),
        "ffn2_w_gate": nrm((L, D, D_FF), D ** -0.5),
        "ffn2_w_up": nrm((L, D, D_FF), D ** -0.5),
        "ffn2_w_down": nrm((L, D_FF, D), D_FF ** -0.5),
    }


def reference(x_prompt, x_sample, cache_sb_k, cache_sb_v, state_rwkv_S, state_rwkv_shift, meta_tokens,
              ffn1_norm_pre, ffn1_norm_post, ffn1_w_gate, ffn1_w_up, ffn1_w_down, mix_norm_pre, mix_norm_post,
              w_in, sb_out_gain, rwkv_mu, rwkv_w0, rwkv_w2, rwkv_a0, rwkv_a2, rwkv_g2, rwkv_k_k, rwkv_k_a,
              rwkv_r_k, rwkv_ln_w, rwkv_ln_b, w_out, ffn2_norm_pre, ffn2_norm_post, ffn2_w_gate, ffn2_w_up,
              ffn2_w_down):
    bp = x_prompt.shape[0]
    meta = jnp.broadcast_to(meta_tokens[None].astype(x_prompt.dtype), (bp, N_META, D_MODEL))
    hp = jnp.concatenate([meta, x_prompt], axis=1)
    hs = x_sample
    S0_p = jnp.zeros((bp, H_RW, HEAD_DIM, HEAD_DIM), state_rwkv_S.dtype)
    shift0_p = jnp.zeros((bp, P_RW), state_rwkv_shift.dtype)
    kp_l, vp_l, Sp_l, shp_l, ks_l, vs_l, Ss_l, shs_l = [], [], [], [], [], [], [], []
    for l in range(DEPTH):
        lw = dict(ffn1_norm_pre=ffn1_norm_pre[l], ffn1_norm_post=ffn1_norm_post[l], ffn1_w_gate=ffn1_w_gate[l],
                  ffn1_w_up=ffn1_w_up[l], ffn1_w_down=ffn1_w_down[l], mix_norm_pre=mix_norm_pre[l],
                  mix_norm_post=mix_norm_post[l], w_in=w_in[l], sb_out_gain=sb_out_gain[l], rwkv_mu=rwkv_mu[l],
                  rwkv_w0=rwkv_w0[l], rwkv_w2=rwkv_w2[l], rwkv_a0=rwkv_a0[l], rwkv_a2=rwkv_a2[l],
                  rwkv_g2=rwkv_g2[l], rwkv_k_k=rwkv_k_k[l], rwkv_k_a=rwkv_k_a[l], rwkv_r_k=rwkv_r_k[l],
                  rwkv_ln_w=rwkv_ln_w[l], rwkv_ln_b=rwkv_ln_b[l], w_out=w_out[l], ffn2_norm_pre=ffn2_norm_pre[l],
                  ffn2_norm_post=ffn2_norm_post[l], ffn2_w_gate=ffn2_w_gate[l], ffn2_w_up=ffn2_w_up[l],
                  ffn2_w_down=ffn2_w_down[l])
        hp, kp, vp, Sp, shp = _layer(hp, None, None, S0_p, shift0_p, lw)
        hs, ks_, vs_, Ss, shs = _layer(hs, cache_sb_k[l], cache_sb_v[l], state_rwkv_S[l], state_rwkv_shift[l], lw)
        kp_l.append(kp); vp_l.append(vp); Sp_l.append(Sp); shp_l.append(shp)
        ks_l.append(ks_); vs_l.append(vs_); Ss_l.append(Ss); shs_l.append(shs)
    y_prompt = hp[:, N_META:]
    return (y_prompt, hs, jnp.stack(kp_l), jnp.stack(vp_l), jnp.stack(Sp_l), jnp.stack(shp_l),
            jnp.stack(ks_l), jnp.stack(vs_l), jnp.stack(Ss_l), jnp.stack(shs_l))
```

```python
import functools

import jax
import jax.numpy as jnp
from jax import lax
from jax.experimental import pallas as pl
from jax.experimental.pallas import tpu as pltpu

F32 = jnp.float32
BF16 = jnp.bfloat16

HEAD_DIM = 64
LANES = 128
NORM_EPS = 1e-6
GN_EPS = 64e-5
KK_EPS = 1e-12
VMEM_LIMIT_BYTES = 56 * 2 ** 20

ATTN_BLOCK = 256
CACHE_BLOCK = 512
SCAN_CHUNK = 64
ROW_TILE = 512
FF_TILE = 512


def _params(*sem):
    return pltpu.CompilerParams(dimension_semantics=sem, vmem_limit_bytes=VMEM_LIMIT_BYTES)


def _tile(n, pref, mult=16):
    t = min(pref, n)
    t -= t % mult
    while t > mult and n % t:
        t -= mult
    assert t >= mult and n % t == 0, (n, pref)
    return t


def _log2(n):
    k = n.bit_length() - 1
    assert 1 << k == n, n
    return k


_NN = (((1,), (0,)), ((), ()))
_NT = (((1,), (1,)), ((), ()))
_TN = (((0,), (0,)), ((), ()))


def _dg(a, b, dims):
    return lax.dot_general(a.astype(BF16), b.astype(BF16), dims, preferred_element_type=F32)


def _split2(x):
    hi = x.astype(BF16)
    lo = (x - hi.astype(F32)).astype(BF16)
    return hi, lo


def _split3(x):
    hi = x.astype(BF16)
    r = x - hi.astype(F32)
    mid = r.astype(BF16)
    lo = (r - mid.astype(F32)).astype(BF16)
    return hi, mid, lo


def _dg3(a, b, dims=_NN):
    ah, al = _split2(a)
    bh, bl = _split2(b)
    return _dg(ah, bh, dims) + (_dg(ah, bl, dims) + _dg(al, bh, dims))


def _dg_sel(x, sel, pieces):
    parts = _split2(x) if pieces == 2 else _split3(x)
    out = _dg(parts[0], sel, _NN)
    for p in parts[1:]:
        out = out + _dg(p, sel, _NN)
    return out


def _sel_dg(sel, x, pieces):
    parts = _split2(x) if pieces == 2 else _split3(x)
    out = _dg(sel, parts[0], _NN)
    for p in parts[1:]:
        out = out + _dg(sel, p, _NN)
    return out


def _rms(x, eps=NORM_EPS):
    return x * lax.rsqrt(jnp.mean(x * x, axis=-1, keepdims=True) + eps)


def _head_pair_sum_matrix(scale=1.0):
    r = lax.broadcasted_iota(jnp.int32, (LANES, LANES), 0) >> _log2(HEAD_DIM)
    c = lax.broadcasted_iota(jnp.int32, (LANES, LANES), 1) >> _log2(HEAD_DIM)
    return jnp.where(r == c, scale, 0.0).astype(BF16)


def _ffn_kernel(x_ref, gpre_ref, gpost_ref, wg_ref, wu_ref, wd_ref, o_ref, u_ref, acc_ref):
    f = pl.program_id(1)

    @pl.when(f == 0)
    def _():
        u_ref[...] = (_rms(x_ref[...]) * gpre_ref[...]).astype(BF16)
        acc_ref[...] = jnp.zeros_like(acc_ref)

    u = u_ref[...]
    gate = jnp.dot(u, wg_ref[...], preferred_element_type=F32)
    up = jnp.dot(u, wu_ref[...], preferred_element_type=F32)
    act = (gate * jax.nn.sigmoid(gate) * up).astype(BF16)
    acc_ref[...] += jnp.dot(act, wd_ref[...], preferred_element_type=F32)

    @pl.when(f == pl.num_programs(1) - 1)
    def _():
        o_ref[...] = x_ref[...] + 0.5 * (_rms(acc_ref[...]) * gpost_ref[...])


def _ffn(x, g_pre, g_post, w_gate, w_up, w_down):
    n, d = x.shape
    d_ff = w_gate.shape[1]
    tm = _tile(n, ROW_TILE)
    tf = _tile(d_ff, FF_TILE, LANES)
    return pl.pallas_call(
        _ffn_kernel,
        grid=(n // tm, d_ff // tf),
        in_specs=[
            pl.BlockSpec((tm, d), lambda i, f: (i, 0)),
            pl.BlockSpec((1, d), lambda i, f: (0, 0)),
            pl.BlockSpec((1, d), lambda i, f: (0, 0)),
            pl.BlockSpec((d, tf), lambda i, f: (0, f)),
            pl.BlockSpec((d, tf), lambda i, f: (0, f)),
            pl.BlockSpec((tf, d), lambda i, f: (f, 0)),
        ],
        out_specs=pl.BlockSpec((tm, d), lambda i, f: (i, 0)),
        out_shape=jax.ShapeDtypeStruct((n, d), F32),
        scratch_shapes=[pltpu.VMEM((tm, d), BF16), pltpu.VMEM((tm, d), F32)],
        compiler_params=_params("parallel", "arbitrary"),
    )(x, g_pre, g_post, w_gate, w_up, w_down)


def _norm_matmul_kernel(x_ref, g_ref, w_ref, o_ref):
    u = (_rms(x_ref[...]) * g_ref[...]).astype(BF16)
    o_ref[...] = jnp.dot(u, w_ref[...], preferred_element_type=F32)


def _norm_matmul(x, g, w, col_tile):
    n, d = x.shape
    cols = w.shape[1]
    tm = _tile(n, ROW_TILE)
    tn = _tile(cols, col_tile, LANES)
    return pl.pallas_call(
        _norm_matmul_kernel,
        grid=(cols // tn, n // tm),
        in_specs=[
            pl.BlockSpec((tm, d), lambda j, i: (i, 0)),
            pl.BlockSpec((1, d), lambda j, i: (0, 0)),
            pl.BlockSpec((d, tn), lambda j, i: (0, j)),
        ],
        out_specs=pl.BlockSpec((tm, tn), lambda j, i: (i, j)),
        out_shape=jax.ShapeDtypeStruct((n, cols), F32),
        compiler_params=_params("parallel", "parallel"),
    )(x, g, w)


def _mix_out_kernel(osb_ref, orw_ref, wsb_ref, wrw_ref, h_ref, g_ref, o_ref):
    o = jnp.dot(osb_ref[...], wsb_ref[...], preferred_element_type=F32)
    o = o + jnp.dot(orw_ref[...], wrw_ref[...], preferred_element_type=F32)
    o_ref[...] = h_ref[...] + _rms(o) * g_ref[...]


def _mix_out(o_sb, o_rw, w_sb, w_rw, h, g):
    n, d = h.shape
    tm = _tile(n, ROW_TILE // 2)
    return pl.pallas_call(
        _mix_out_kernel,
        grid=(n // tm,),
        in_specs=[
            pl.BlockSpec((tm, o_sb.shape[1]), lambda i: (i, 0)),
            pl.BlockSpec((tm, o_rw.shape[1]), lambda i: (i, 0)),
            pl.BlockSpec(w_sb.shape, lambda i: (0, 0)),
            pl.BlockSpec(w_rw.shape, lambda i: (0, 0)),
            pl.BlockSpec((tm, d), lambda i: (i, 0)),
            pl.BlockSpec((1, d), lambda i: (0, 0)),
        ],
        out_specs=pl.BlockSpec((tm, d), lambda i: (i, 0)),
        out_shape=jax.ShapeDtypeStruct((n, d), F32),
        compiler_params=_params("parallel"),
    )(o_sb, o_rw, w_sb, w_rw, h, g)


def _strict_upper(n):
    r = lax.broadcasted_iota(jnp.int32, (n, n), 0)
    c = lax.broadcasted_iota(jnp.int32, (n, n), 1)
    return jnp.where(r > c, 1.0, 0.0).astype(BF16)


def _sb_weights(z, mask, run, tri):
    sp = jnp.maximum(z, 0.0) + jnp.log1p(jnp.exp(-jnp.abs(z)))
    log_stay = -sp if mask is None else jnp.where(mask, -sp, 0.0)
    between = _dg_sel(log_stay, tri, 2)
    a = jnp.exp((z - sp) + between + run)
    if mask is not None:
        a = jnp.where(mask, a, 0.0)
    return a, run + jnp.sum(log_stay, axis=-1, keepdims=True)


def _head_rms(o, gain, first):
    sq = o * o
    s0 = jnp.sum(jnp.where(first, sq, 0.0), axis=-1, keepdims=True)
    s1 = jnp.sum(jnp.where(first, 0.0, sq), axis=-1, keepdims=True)
    ms = jnp.where(first, s0, s1) * (1.0 / HEAD_DIM)
    return o * lax.rsqrt(ms + NORM_EPS) * gain


def _sb_prompt_kernel(q_ref, k_ref, v_ref, tri_ref, gain_ref, o_ref, acc_ref, run_ref, *, blk):
    qi = pl.program_id(2)
    q = q_ref[0] * (HEAD_DIM ** -0.5)
    first = lax.broadcasted_iota(jnp.int32, (1, LANES), 1) < HEAD_DIM
    q_heads = (jnp.where(first, q, 0.0).astype(BF16), jnp.where(first, 0.0, q).astype(BF16))
    tri = tri_ref[...]
    causal = (lax.broadcasted_iota(jnp.int32, (blk, blk), 1)
              < lax.broadcasted_iota(jnp.int32, (blk, blk), 0))

    acc_ref[...] = jnp.zeros_like(acc_ref)
    run_ref[...] = jnp.zeros_like(run_ref)

    def sweep(kb, mask):
        start = pl.multiple_of(kb * blk, blk)
        k_blk = k_ref[0, pl.ds(start, blk), :].astype(BF16)
        v_blk = v_ref[0, pl.ds(start, blk), :].astype(BF16)
        for h in range(2):
            z = _dg(q_heads[h], k_blk, _NT)
            a, run = _sb_weights(z, mask, run_ref[h], tri)
            run_ref[h] = run
            acc_ref[h] += jnp.dot(a.astype(BF16), v_blk, preferred_element_type=F32)

    sweep(qi, causal)

    def body(i, carry):
        sweep(qi - 1 - i, None)
        return carry

    lax.fori_loop(0, qi, body, 0)
    o = jnp.where(first, acc_ref[0], acc_ref[1])
    o_ref[0] = _head_rms(o, gain_ref[0], first).astype(o_ref.dtype)


def _sb_prompt(p_sb, gain, tri):
    b, t, w = p_sb.shape
    d_sb = w // 3
    pairs = d_sb // LANES
    blk = ATTN_BLOCK
    assert t % blk == 0
    return pl.pallas_call(
        functools.partial(_sb_prompt_kernel, blk=blk),
        grid=(b, pairs, t // blk),
        in_specs=[
            pl.BlockSpec((1, blk, LANES), lambda bi, p, qi: (bi, qi, p)),
            pl.BlockSpec((1, t, LANES), lambda bi, p, qi: (bi, 0, pairs + p)),
            pl.BlockSpec((1, t, LANES), lambda bi, p, qi: (bi, 0, 2 * pairs + p)),
            pl.BlockSpec((blk, blk), lambda bi, p, qi: (0, 0)),
            pl.BlockSpec((1, 1, LANES), lambda bi, p, qi: (p, 0, 0)),
        ],
        out_specs=pl.BlockSpec((1, blk, LANES), lambda bi, p, qi: (bi, qi, p)),
        out_shape=jax.ShapeDtypeStruct((b, t, d_sb), BF16),
        scratch_shapes=[pltpu.VMEM((2, blk, LANES), F32), pltpu.VMEM((2, blk, 1), F32)],
        compiler_params=_params("parallel", "parallel", "arbitrary"),
    )(p_sb, p_sb, p_sb, tri, gain)


def _sb_sample_kernel(q_ref, kn_ref, vn_ref, kc_ref, vc_ref, tri_ref, gain_ref, o_ref,
                      qs_ref, acc_ref, run_ref, *, n_q, n_heads):
    j = pl.program_id(1)
    rows = n_heads * n_q
    d_sb = n_heads * HEAD_DIM
    row_head = lax.broadcasted_iota(jnp.int32, (rows, 1), 0) >> _log2(n_q)
    tri = tri_ref[...]

    @pl.when(j == 0)
    def _():
        lane_head = lax.broadcasted_iota(jnp.int32, (1, d_sb), 1) >> _log2(HEAD_DIM)
        q = jnp.tile(q_ref[0] * (HEAD_DIM ** -0.5), (n_heads, 1))
        qs = jnp.where(row_head == lane_head, q, 0.0).astype(BF16)
        qs_ref[...] = qs
        n_new = kn_ref.shape[1]
        q_pos = lax.broadcasted_iota(jnp.int32, (rows, 1), 0) & (n_q - 1)
        mask = lax.broadcasted_iota(jnp.int32, (1, n_new), 1) < q_pos
        z = _dg(qs, kn_ref[0], _NT)
        a, run = _sb_weights(z, mask, jnp.zeros((rows, 1), F32), tri_ref[:n_new, :n_new])
        run_ref[...] = run
        acc_ref[...] = jnp.dot(a.astype(BF16), vn_ref[0].astype(BF16), preferred_element_type=F32)

    z = _dg(qs_ref[...], kc_ref[0], _NT)
    a, run = _sb_weights(z, None, run_ref[...], tri)
    run_ref[...] = run
    acc_ref[...] += jnp.dot(a.astype(BF16), vc_ref[0].astype(BF16), preferred_element_type=F32)

    @pl.when(j == pl.num_programs(1) - 1)
    def _():
        lane_head = lax.broadcasted_iota(jnp.int32, (1, d_sb), 1) >> _log2(HEAD_DIM)
        o = jnp.zeros((n_q, d_sb), F32)
        for h in range(n_heads):
            o = o + jnp.where(lane_head == h, acc_ref[h * n_q:(h + 1) * n_q, :], 0.0)
        first = (lax.broadcasted_iota(jnp.int32, (1, LANES), 1) < HEAD_DIM)
        for p in range(d_sb // LANES):
            sl = slice(p * LANES, (p + 1) * LANES)
            o_ref[0, :, sl] = _head_rms(o[:, sl], gain_ref[:, sl], first).astype(o_ref.dtype)


def _sb_sample(p_sb, cache_k, cache_v, gain, tri):
    b, n_q, w = p_sb.shape
    d_sb = w // 3
    n_heads = d_sb // HEAD_DIM
    past = cache_k.shape[1]
    blk = tri.shape[0]
    assert past % blk == 0 and n_q <= LANES
    n_blocks = past // blk
    pad = ((0, 0), (0, LANES - n_q), (0, 0))
    k_new = jnp.pad(p_sb[:, :, d_sb:2 * d_sb], pad)
    v_new = jnp.pad(p_sb[:, :, 2 * d_sb:], pad)
    rows = n_heads * n_q
    return pl.pallas_call(
        functools.partial(_sb_sample_kernel, n_q=n_q, n_heads=n_heads),
        grid=(b, n_blocks),
        in_specs=[
            pl.BlockSpec((1, n_q, d_sb), lambda bi, j: (bi, 0, 0)),
            pl.BlockSpec((1, LANES, d_sb), lambda bi, j: (bi, 0, 0)),
            pl.BlockSpec((1, LANES, d_sb), lambda bi, j: (bi, 0, 0)),
            pl.BlockSpec((1, blk, d_sb), lambda bi, j: (bi, n_blocks - 1 - j, 0)),
            pl.BlockSpec((1, blk, d_sb), lambda bi, j: (bi, n_blocks - 1 - j, 0)),
            pl.BlockSpec((blk, blk), lambda bi, j: (0, 0)),
            pl.BlockSpec((1, d_sb), lambda bi, j: (0, 0)),
        ],
        out_specs=pl.BlockSpec((1, n_q, d_sb), lambda bi, j: (bi, 0, 0)),
        out_shape=jax.ShapeDtypeStruct((b, n_q, d_sb), BF16),
        scratch_shapes=[pltpu.VMEM((rows, d_sb), BF16), pltpu.VMEM((rows, d_sb), F32),
                        pltpu.VMEM((rows, 1), F32)],
        compiler_params=_params("parallel", "arbitrary"),
    )(p_sb, k_new, v_new, cache_k, cache_v, tri, gain)


def _rwkv_prep_kernel(p_ref, tail_ref, shift_ref, mu_ref, w0_ref, w2_ref, a0_ref, a2_ref, g2_ref,
                      r_ref, k_ref, v_ref, lw_ref, al_ref, g_ref, *, d_rw, wd, wa):
    i = pl.program_id(1)
    p = p_ref[0]
    tm = p.shape[0]
    before = jnp.where(i == 0, shift_ref[0], tail_ref[0, 7:8, :])
    row = lax.broadcasted_iota(jnp.int32, (tm, 1), 0)
    prev = jnp.where(row == 0, before, pltpu.roll(p, 1, 0))
    xs = p + (prev - p) * mu_ref[...]
    o1, o2, o3 = d_rw, 2 * d_rw, 3 * d_rw
    r_ref[0] = xs[:, :o1]
    k_ref[0] = xs[:, o1:o2]
    v_ref[0] = xs[:, o2:o3]
    dw = xs[:, o3:o3 + wd]
    da = xs[:, o3 + wd:o3 + wd + wa]
    dg = xs[:, o3 + wd + wa:]
    w_pre = w0_ref[...] + _dg(jnp.tanh(dw), w2_ref[...], _NN)
    w_log = -(jnp.maximum(-w_pre, 0.0) + jnp.log1p(jnp.exp(-jnp.abs(w_pre)))) - 0.5
    lw_ref[0] = -jnp.exp(w_log)
    al_ref[0] = jax.nn.sigmoid(a0_ref[...] + _dg(da, a2_ref[...], _NN))
    g_ref[0] = _dg(jax.nn.sigmoid(dg), g2_ref[...], _NN)


def _rwkv_prep(p_rw, shift0, mu, w0, w2, a0, a2, g2):
    b, t, pw = p_rw.shape
    d_rw = w0.shape[1]
    wd, wa = w2.shape[0], a2.shape[0]
    tm = _tile(t, ROW_TILE // 2)
    tail = max(tm // 8, 1)
    out = jax.ShapeDtypeStruct((b, t, d_rw), F32)
    row_spec = pl.BlockSpec((1, tm, d_rw), lambda bi, i: (bi, i, 0))
    full = lambda a: pl.BlockSpec(a.shape, lambda bi, i: (0,) * a.ndim)
    return pl.pallas_call(
        functools.partial(_rwkv_prep_kernel, d_rw=d_rw, wd=wd, wa=wa),
        grid=(b, t // tm),
        in_specs=[
            pl.BlockSpec((1, tm, pw), lambda bi, i: (bi, i, 0)),
            pl.BlockSpec((1, 8, pw), lambda bi, i: (bi, jnp.maximum(i * tail - 1, 0), 0)),
            pl.BlockSpec((1, 1, pw), lambda bi, i: (bi, 0, 0)),
            full(mu), full(w0), full(w2), full(a0), full(a2), full(g2),
        ],
        out_specs=[row_spec] * 6,
        out_shape=[out] * 6,
        compiler_params=_params("parallel", "parallel"),
    )(p_rw, p_rw, shift0, mu, w0, w2, a0, a2, g2)


def _stack(x, first):
    return jnp.concatenate([jnp.where(first, x, 0.0), jnp.where(first, 0.0, x)], axis=0)


def _unit_lower_inverse(low, c):
    n = low.shape[0]
    r = lax.broadcasted_iota(jnp.int32, (n, n), 0)
    col = lax.broadcasted_iota(jnp.int32, (n, n), 1)
    inv = jnp.where(r == col, 1.0, 0.0)
    for lvl in range(_log2(c)):
        same = (r >> (lvl + 1)) == (col >> (lvl + 1))
        lower_left = same & (((r >> lvl) & 1) == 1) & (((col >> lvl) & 1) == 0)
        off = jnp.where(lower_left, low, 0.0)
        inv = inv + _dg3(_dg3(inv, off), inv)
    return inv


def _rwkv_scan_kernel(r_ref, k_ref, v_ref, lw_ref, al_ref, g_ref, kk_ref, ka_ref, rk_ref,
                      lnw_ref, lnb_ref, s0_ref, o_ref, sT_ref, s_ref, *, c, t_valid):
    ci = pl.program_id(2)

    @pl.when(ci == 0)
    def _():
        s_ref[...] = s0_ref[0, 0]

    first = lax.broadcasted_iota(jnp.int32, (1, LANES), 1) < HEAD_DIM
    valid = (ci * c + lax.broadcasted_iota(jnp.int32, (c, 1), 0)) < t_valid
    r = r_ref[0]
    k = jnp.where(valid, k_ref[0], 0.0)
    v = jnp.where(valid, v_ref[0], 0.0)
    lw = jnp.where(valid, lw_ref[0], 0.0)
    al = al_ref[0]
    pair_sum = _head_pair_sum_matrix()

    kk = k * kk_ref[...]
    kk = kk / jnp.maximum(jnp.sqrt(_dg_sel(kk * kk, pair_sum, 3)), KK_EPS)
    k_mod = k * (1.0 + (al - 1.0) * ka_ref[...])
    a = -kk
    b = kk * al

    t_i = lax.broadcasted_iota(jnp.int32, (c, c), 0)
    s_i = lax.broadcasted_iota(jnp.int32, (c, c), 1)
    cum = _sel_dg(jnp.where(s_i <= t_i, 1.0, 0.0).astype(BF16), lw, 3)
    cum_end = cum[c - 1:c, :]
    dec_in = jnp.exp(cum)
    dec_out = jnp.exp(-cum)
    dec_rest = jnp.exp(cum_end - cum)

    a_t = _stack(a * jnp.exp(cum - lw), first)
    r_t = _stack(r * dec_in, first)
    b_t = _stack(b * dec_out, first)
    k_t = _stack(k_mod * dec_out, first)
    b_h = _stack(b * dec_rest, first)
    k_h = _stack(k_mod * dec_rest, first)
    v_s = _stack(v, first)

    n = 2 * c
    prod = _dg3(jnp.concatenate([a_t, r_t], axis=0), jnp.concatenate([b_t, k_t], axis=0), _NT)
    tt = lax.broadcasted_iota(jnp.int32, (n, n), 0) & (c - 1)
    ss = lax.broadcasted_iota(jnp.int32, (n, n), 1) & (c - 1)
    strict, incl = ss < tt, ss <= tt
    l_ab = jnp.where(strict, prod[:n, :n], 0.0)
    l_ak = jnp.where(strict, prod[:n, n:], 0.0)
    m_rb = jnp.where(incl, prod[n:, :n], 0.0)
    m_rk = jnp.where(incl, prod[n:, n:], 0.0)

    solve = _unit_lower_inverse(l_ab, c)
    au = _dg3(solve, jnp.concatenate([a_t, _dg3(l_ak, v_s)], axis=1))
    a_hat, u0 = au[:, :LANES], au[:, LANES:]
    r_hat = r_t + _dg3(m_rb, a_hat)
    y0 = _dg3(m_rb, u0) + _dg3(m_rk, v_s)
    diag = (lax.broadcasted_iota(jnp.int32, (LANES, LANES), 0)
            == lax.broadcasted_iota(jnp.int32, (LANES, LANES), 1))
    g_mat = jnp.where(diag, jnp.exp(cum_end), 0.0) + _dg3(b_h, a_hat, _TN)
    h_mat = _dg3(jnp.concatenate([b_h, k_h], axis=0), jnp.concatenate([u0, v_s], axis=0), _TN)

    s_old = s_ref[...]
    step = _dg3(jnp.concatenate([r_hat, g_mat], axis=0), s_old)
    y_s = step[:n] + y0
    s_ref[...] = step[n:] + h_mat
    y = y_s[:c] + y_s[c:]

    pair_mean = _head_pair_sum_matrix(1.0 / HEAD_DIM)
    mean = _dg_sel(y, pair_mean, 2)
    d = y - mean
    var = _dg_sel(d * d, pair_mean, 2)
    y = d * lax.rsqrt(var + GN_EPS) * lnw_ref[...] + lnb_ref[...]
    bonus = _dg_sel(r * k_mod * rk_ref[...], pair_sum, 2) * v
    o_ref[0] = ((y + bonus) * g_ref[0]).astype(o_ref.dtype)

    @pl.when(ci == pl.num_programs(2) - 1)
    def _():
        sT_ref[0, 0] = s_ref[...]


def _rwkv_scan(r, k, v, lw, al, g, k_k, k_a, r_k, ln_w, ln_b, s0, t_valid):
    b, t_in, d_rw = r.shape
    pairs = d_rw // LANES
    c = SCAN_CHUNK
    t = -(-t_in // c) * c
    if t != t_in:
        r, k, v, lw, al, g = (jnp.pad(x, ((0, 0), (0, t - t_in), (0, 0))) for x in (r, k, v, lw, al, g))
    row_spec = pl.BlockSpec((1, c, LANES), lambda bi, p, ci: (bi, ci, p))
    par_spec = pl.BlockSpec((1, LANES), lambda bi, p, ci: (0, p))
    st_spec = pl.BlockSpec((1, 1, LANES, LANES), lambda bi, p, ci: (bi, p, 0, 0))
    o, s_t = pl.pallas_call(
        functools.partial(_rwkv_scan_kernel, c=c, t_valid=t_valid),
        grid=(b, pairs, t // c),
        in_specs=[row_spec] * 6 + [par_spec] * 5 + [st_spec],
        out_specs=[row_spec, st_spec],
        out_shape=[jax.ShapeDtypeStruct((b, t, d_rw), BF16),
                   jax.ShapeDtypeStruct((b, pairs, LANES, LANES), F32)],
        scratch_shapes=[pltpu.VMEM((LANES, LANES), F32)],
        compiler_params=_params("parallel", "parallel", "arbitrary"),
    )(r, k, v, lw, al, g, k_k, k_a, r_k, ln_w, ln_b, s0)
    return o[:, :t_in], s_t


def _state_to_pairs(s):
    b, h, e, _ = s.shape
    st = jnp.swapaxes(s, -1, -2).reshape(b, h // 2, 2, e, e)
    z = jnp.zeros_like(st[:, :, 0])
    top = jnp.concatenate([st[:, :, 0], z], axis=-1)
    bot = jnp.concatenate([z, st[:, :, 1]], axis=-1)
    return jnp.concatenate([top, bot], axis=-2)


def _pairs_to_state(sp):
    b, pairs = sp.shape[:2]
    e = HEAD_DIM
    blocks = jnp.stack([sp[:, :, :e, :e], sp[:, :, e:, e:]], axis=2)
    return jnp.swapaxes(blocks, -1, -2).reshape(b, 2 * pairs, e, e)


def _pad_cols(x, width):
    return jnp.pad(x, [(0, 0)] * (x.ndim - 1) + [(0, width - x.shape[-1])])


def _rw_layout(x, d_rw, ranks, widths):
    parts = [x[..., :3 * d_rw]]
    o = 3 * d_rw
    for rank, width in zip(ranks, widths):
        parts.append(_pad_cols(x[..., o:o + rank], width))
        o += rank
    return jnp.concatenate(parts, axis=-1)


def _rw_unlayout(x, d_rw, ranks, widths):
    parts = [x[..., :3 * d_rw]]
    o = 3 * d_rw
    for rank, width in zip(ranks, widths):
        parts.append(x[..., o:o + rank])
        o += width
    return jnp.concatenate(parts, axis=-1)


def _layer(h, lw, cache, s0_pairs, shift0, t_valid, tri):
    b, t, d = h.shape
    d_sb = lw["sb_gain"].shape[-1]
    x = h.reshape(b * t, d)
    x = _ffn(x, lw["ffn1_pre"], lw["ffn1_post"], lw["ffn1_gate"], lw["ffn1_up"], lw["ffn1_down"])
    p_sb = _norm_matmul(x, lw["mix_pre"], lw["w_in_sb"], 1024).reshape(b, t, -1)
    p_rw = _norm_matmul(x, lw["mix_pre"], lw["w_in_rw"], 1280).reshape(b, t, -1)
    if cache is None:
        o_sb = _sb_prompt(p_sb, lw["sb_gain_pairs"], tri)
    else:
        o_sb = _sb_sample(p_sb, cache[0], cache[1], lw["sb_gain"], tri)
    r, k, v, lwd, al, g = _rwkv_prep(p_rw, shift0, lw["mu"], lw["w0"], lw["w2"], lw["a0"], lw["a2"],
                                     lw["g2"])
    o_rw, s_t = _rwkv_scan(r, k, v, lwd, al, g, lw["k_k"], lw["k_a"], lw["r_k"], lw["ln_w"],
                           lw["ln_b"], s0_pairs, t_valid)
    x = _mix_out(o_sb.reshape(b * t, -1), o_rw.reshape(b * t, -1), lw["w_out_sb"], lw["w_out_rw"],
                 x, lw["mix_post"])
    x = _ffn(x, lw["ffn2_pre"], lw["ffn2_post"], lw["ffn2_gate"], lw["ffn2_up"], lw["ffn2_down"])
    k_new = p_sb[:, :t_valid, d_sb:2 * d_sb]
    v_new = p_sb[:, :t_valid, 2 * d_sb:]
    return x.reshape(b, t, d), k_new, v_new, s_t, p_rw[:, t_valid - 1]


def kernel(x_prompt, x_sample, cache_sb_k, cache_sb_v, state_rwkv_S, state_rwkv_shift, meta_tokens, ffn1_norm_pre, ffn1_norm_post, ffn1_w_gate, ffn1_w_up, ffn1_w_down, mix_norm_pre, mix_norm_post, w_in, sb_out_gain, rwkv_mu, rwkv_w0, rwkv_w2, rwkv_a0, rwkv_a2, rwkv_g2, rwkv_k_k, rwkv_k_a, rwkv_r_k, rwkv_ln_w, rwkv_ln_b, w_out, ffn2_norm_pre, ffn2_norm_post, ffn2_w_gate, ffn2_w_up, ffn2_w_down):
    depth, d = ffn1_norm_pre.shape
    bp, seq, _ = x_prompt.shape
    bs, dec_seq, _ = x_sample.shape
    n_meta = meta_tokens.shape[0]
    h_sb = sb_out_gain.shape[1]
    d_sb = h_sb * HEAD_DIM
    d_rw = rwkv_w0.shape[1]
    h_rw = d_rw // HEAD_DIM
    ranks = (rwkv_w2.shape[1], rwkv_a2.shape[1], rwkv_g2.shape[1])
    widths = tuple(-(-r // LANES) * LANES for r in ranks)
    p_rw_cols = 3 * d_rw + sum(ranks)
    past = cache_sb_k.shape[2]

    t_valid = n_meta + seq
    t_pad = -(-t_valid // ATTN_BLOCK) * ATTN_BLOCK
    meta = jnp.broadcast_to(meta_tokens[None].astype(x_prompt.dtype), (bp, n_meta, d))
    hp = jnp.concatenate([meta, x_prompt, jnp.zeros((bp, t_pad - t_valid, d), x_prompt.dtype)], axis=1)
    hs = x_sample

    tri_p = _strict_upper(ATTN_BLOCK)
    tri_s = _strict_upper(min(CACHE_BLOCK, past))
    row = lambda x: x.reshape(1, -1)
    pad_rows = lambda w, width: jnp.pad(w, ((0, width - w.shape[0]), (0, 0)))
    w_in_rw_all = _rw_layout(w_in[:, :, 3 * d_sb:], d_rw, ranks, widths).astype(BF16)
    w_in_sb_all = w_in[:, :, :3 * d_sb].astype(BF16)
    mu_all = _rw_layout(rwkv_mu, d_rw, ranks, widths)

    outs = [[] for _ in range(8)]
    s0_p = jnp.zeros((bp, h_rw // 2, LANES, LANES), F32)
    shift0_p = jnp.zeros((bp, 1, w_in_rw_all.shape[-1]), F32)
    for l in range(depth):
        lw = dict(
            ffn1_pre=row(ffn1_norm_pre[l]), ffn1_post=row(ffn1_norm_post[l]),
            ffn1_gate=ffn1_w_gate[l].astype(BF16), ffn1_up=ffn1_w_up[l].astype(BF16),
            ffn1_down=ffn1_w_down[l].astype(BF16),
            ffn2_pre=row(ffn2_norm_pre[l]), ffn2_post=row(ffn2_norm_post[l]),
            ffn2_gate=ffn2_w_gate[l].astype(BF16), ffn2_up=ffn2_w_up[l].astype(BF16),
            ffn2_down=ffn2_w_down[l].astype(BF16),
            mix_pre=row(mix_norm_pre[l]), mix_post=row(mix_norm_post[l]),
            w_in_sb=w_in_sb_all[l], w_in_rw=w_in_rw_all[l],
            sb_gain=row(sb_out_gain[l]), sb_gain_pairs=sb_out_gain[l].reshape(h_sb // 2, 1, LANES),
            mu=row(mu_all[l]), w0=row(rwkv_w0[l]), a0=row(rwkv_a0[l]),
            w2=pad_rows(rwkv_w2[l], widths[0]).astype(BF16),
            a2=pad_rows(rwkv_a2[l], widths[1]).astype(BF16),
            g2=pad_rows(rwkv_g2[l], widths[2]).astype(BF16),
            k_k=row(rwkv_k_k[l]), k_a=row(rwkv_k_a[l]), r_k=row(rwkv_r_k[l]),
            ln_w=row(rwkv_ln_w[l]), ln_b=row(rwkv_ln_b[l]),
            w_out_sb=w_out[l, :d_sb].astype(BF16), w_out_rw=w_out[l, d_sb:].astype(BF16),
        )
        hp, kp, vp, sp, shp = _layer(hp, lw, None, s0_p, shift0_p, t_valid, tri_p)
        cache = (cache_sb_k[l].reshape(bs, past, d_sb), cache_sb_v[l].reshape(bs, past, d_sb))
        shift0_s = _rw_layout(state_rwkv_shift[l], d_rw, ranks, widths)[:, None, :]
        hs, ks, vs, ss, shs = _layer(hs, lw, cache, _state_to_pairs(state_rwkv_S[l].astype(F32)),
                                     shift0_s, dec_seq, tri_s)
        per_layer = (
            kp.reshape(bp, t_valid, h_sb, HEAD_DIM), vp.reshape(bp, t_valid, h_sb, HEAD_DIM),
            _pairs_to_state(sp).astype(state_rwkv_S.dtype),
            _rw_unlayout(shp, d_rw, ranks, widths).astype(state_rwkv_shift.dtype),
            ks.reshape(bs, dec_seq, h_sb, HEAD_DIM), vs.reshape(bs, dec_seq, h_sb, HEAD_DIM),
            _pairs_to_state(ss).astype(state_rwkv_S.dtype),
            _rw_unlayout(shs, d_rw, ranks, widths).astype(state_rwkv_shift.dtype),
        )
        for acc, val in zip(outs, per_layer):
            acc.append(val)
    assert p_rw_cols == state_rwkv_shift.shape[-1]
    y_prompt = hp[:, n_meta:t_valid]
    return (y_prompt, hs) + tuple(jnp.stack(o) for o in outs)
```

```python
import functools

import jax
import jax.numpy as jnp
from jax import lax
from jax.experimental import pallas as pl
from jax.experimental.pallas import tpu as pltpu

F32 = jnp.float32
BF16 = jnp.bfloat16

HEAD_DIM = 64
LANES = 128
NORM_EPS = 1e-6
GN_EPS = 64e-5
KK_EPS = 1e-12
VMEM_LIMIT_BYTES = 56 * 2 ** 20

LOG2E = 1.4426950408889634
ATTN_BLOCK = 256
ATTN_SPLIT = 1
ATTN_UNROLL = 2
CACHE_BLOCK = 512
SCAN_CHUNK = 64
SCAN_GROUP = 8
ROW_TILE = 512
FF_TILE = 512
CAST_BLOCK_BYTES = 8 * 2 ** 20


def _params(*sem):
    return pltpu.CompilerParams(dimension_semantics=sem, vmem_limit_bytes=VMEM_LIMIT_BYTES)


def _tile(n, pref, mult=16):
    t = min(pref, n)
    t -= t % mult
    while t > mult and n % t:
        t -= mult
    assert t >= mult and n % t == 0, (n, pref)
    return t


def _log2(n):
    k = n.bit_length() - 1
    assert 1 << k == n, n
    return k


_NN = (((1,), (0,)), ((), ()))
_NT = (((1,), (1,)), ((), ()))
_TN = (((0,), (0,)), ((), ()))


def _dg(a, b, dims):
    return lax.dot_general(a.astype(BF16), b.astype(BF16), dims, preferred_element_type=F32)


def _split2(x):
    hi = x.astype(BF16)
    lo = (x - hi.astype(F32)).astype(BF16)
    return hi, lo


def _split3(x):
    hi = x.astype(BF16)
    r = x - hi.astype(F32)
    mid = r.astype(BF16)
    lo = (r - mid.astype(F32)).astype(BF16)
    return hi, mid, lo


def _dg3(a, b, dims=_NN):
    ah, al = _split2(a)
    bh, bl = _split2(b)
    return _dg(ah, bh, dims) + (_dg(ah, bl, dims) + _dg(al, bh, dims))


def _dg_sel(x, sel, pieces):
    parts = _split2(x) if pieces == 2 else _split3(x)
    out = _dg(parts[0], sel, _NN)
    for p in parts[1:]:
        out = out + _dg(p, sel, _NN)
    return out


def _sel_dg(sel, x, pieces):
    parts = _split2(x) if pieces == 2 else _split3(x)
    out = _dg(sel, parts[0], _NN)
    for p in parts[1:]:
        out = out + _dg(sel, p, _NN)
    return out


def _rms(x, eps=NORM_EPS):
    return x * lax.rsqrt(jnp.mean(x * x, axis=-1, keepdims=True) + eps)


def _head_pair_sum_matrix(scale=1.0):
    r = lax.broadcasted_iota(jnp.int32, (LANES, LANES), 0) >> _log2(HEAD_DIM)
    c = lax.broadcasted_iota(jnp.int32, (LANES, LANES), 1) >> _log2(HEAD_DIM)
    return jnp.where(r == c, scale, 0.0).astype(BF16)


def _cast_kernel(x_ref, o_ref):
    o_ref[...] = x_ref[...].astype(o_ref.dtype)


def _to_bf16(x):
    flat = x.reshape(-1, x.shape[-1])
    n, w = flat.shape
    tm = _tile(n, max(16, CAST_BLOCK_BYTES // (4 * w)))
    out = pl.pallas_call(
        _cast_kernel,
        grid=(n // tm,),
        in_specs=[pl.BlockSpec((tm, w), lambda i: (i, 0))],
        out_specs=pl.BlockSpec((tm, w), lambda i: (i, 0)),
        out_shape=jax.ShapeDtypeStruct((n, w), BF16),
        compiler_params=_params("parallel"),
        name="to_bf16",
    )(flat)
    return out.reshape(x.shape)


def _ffn_kernel(x_ref, gpre_ref, gpost_ref, wg_ref, wu_ref, wd_ref, o_ref, u_ref, acc_ref):
    f = pl.program_id(1)

    @pl.when(f == 0)
    def _():
        u_ref[...] = (_rms(x_ref[...]) * gpre_ref[...]).astype(BF16)
        acc_ref[...] = jnp.zeros_like(acc_ref)

    u = u_ref[...]
    gate = jnp.dot(u, wg_ref[...], preferred_element_type=F32)
    up = jnp.dot(u, wu_ref[...], preferred_element_type=F32)
    act = (gate * jax.nn.sigmoid(gate) * up).astype(BF16)
    acc_ref[...] += jnp.dot(act, wd_ref[...], preferred_element_type=F32)

    @pl.when(f == pl.num_programs(1) - 1)
    def _():
        o_ref[...] = x_ref[...] + 0.5 * (_rms(acc_ref[...]) * gpost_ref[...])


def _ffn(x, g_pre, g_post, w_gate, w_up, w_down):
    n, d = x.shape
    d_ff = w_gate.shape[1]
    tm = _tile(n, ROW_TILE)
    tf = _tile(d_ff, FF_TILE, LANES)
    return pl.pallas_call(
        _ffn_kernel,
        grid=(n // tm, d_ff // tf),
        in_specs=[
            pl.BlockSpec((tm, d), lambda i, f: (i, 0)),
            pl.BlockSpec((1, d), lambda i, f: (0, 0)),
            pl.BlockSpec((1, d), lambda i, f: (0, 0)),
            pl.BlockSpec((d, tf), lambda i, f: (0, f)),
            pl.BlockSpec((d, tf), lambda i, f: (0, f)),
            pl.BlockSpec((tf, d), lambda i, f: (f, 0)),
        ],
        out_specs=pl.BlockSpec((tm, d), lambda i, f: (i, 0)),
        out_shape=jax.ShapeDtypeStruct((n, d), F32),
        scratch_shapes=[pltpu.VMEM((tm, d), BF16), pltpu.VMEM((tm, d), F32)],
        compiler_params=_params("parallel", "arbitrary"),
        name="ffn_half",
    )(x, g_pre, g_post, w_gate, w_up, w_down)


def _norm_matmul_kernel(x_ref, g_ref, w_ref, o_ref):
    u = (_rms(x_ref[...]) * g_ref[...]).astype(BF16)
    o_ref[...] = jnp.dot(u, w_ref[...], preferred_element_type=F32)


def _norm_matmul(x, g, w, col_tile):
    n, d = x.shape
    cols = w.shape[1]
    tm = _tile(n, ROW_TILE)
    tn = _tile(cols, col_tile, LANES)
    return pl.pallas_call(
        _norm_matmul_kernel,
        grid=(cols // tn, n // tm),
        in_specs=[
            pl.BlockSpec((tm, d), lambda j, i: (i, 0)),
            pl.BlockSpec((1, d), lambda j, i: (0, 0)),
            pl.BlockSpec((d, tn), lambda j, i: (0, j)),
        ],
        out_specs=pl.BlockSpec((tm, tn), lambda j, i: (i, j)),
        out_shape=jax.ShapeDtypeStruct((n, cols), F32),
        compiler_params=_params("parallel", "parallel"),
        name="mix_in",
    )(x, g, w)


def _mix_out_kernel(osb_ref, orw_ref, wsb_ref, wrw_ref, h_ref, g_ref, o_ref):
    o = jnp.dot(osb_ref[...], wsb_ref[...], preferred_element_type=F32)
    o = o + jnp.dot(orw_ref[...], wrw_ref[...], preferred_element_type=F32)
    o_ref[...] = h_ref[...] + _rms(o) * g_ref[...]


def _mix_out(o_sb, o_rw, w_sb, w_rw, h, g):
    n, d = h.shape
    tm = _tile(n, ROW_TILE // 2)
    return pl.pallas_call(
        _mix_out_kernel,
        grid=(n // tm,),
        in_specs=[
            pl.BlockSpec((tm, o_sb.shape[1]), lambda i: (i, 0)),
            pl.BlockSpec((tm, o_rw.shape[1]), lambda i: (i, 0)),
            pl.BlockSpec(w_sb.shape, lambda i: (0, 0)),
            pl.BlockSpec(w_rw.shape, lambda i: (0, 0)),
            pl.BlockSpec((tm, d), lambda i: (i, 0)),
            pl.BlockSpec((1, d), lambda i: (0, 0)),
        ],
        out_specs=pl.BlockSpec((tm, d), lambda i: (i, 0)),
        out_shape=jax.ShapeDtypeStruct((n, d), F32),
        compiler_params=_params("parallel"),
        name="mix_out",
    )(o_sb, o_rw, w_sb, w_rw, h, g)


def _strict_upper(n):
    r = lax.broadcasted_iota(jnp.int32, (n, n), 0)
    c = lax.broadcasted_iota(jnp.int32, (n, n), 1)
    return jnp.where(r > c, 1.0, 0.0).astype(BF16)


def _split_top16(x):
    bits = lax.bitcast_convert_type(x, jnp.uint32) & jnp.uint32(0xFFFF0000)
    hi = lax.bitcast_convert_type(bits, F32)
    return hi.astype(BF16), (x - hi).astype(BF16)


def _sb_sweep(qs, kv_blocks, tri, runs, masks):
    runs = list(runs)
    jobs = []
    for k_blk, v_blk in kv_blocks:
        for c, (q, mask) in enumerate(zip(qs, masks)):
            z = _dg(q, k_blk, _NT)
            soft = jnp.log(1.0 + jnp.exp(jnp.minimum(z, -z)))
            stay = jnp.minimum(-z, 0.0) - soft
            if mask is not None:
                stay = jnp.where(mask, stay, 0.0)
            stacked = jnp.concatenate(_split_top16(stay), axis=0)
            jobs.append((c, v_blk, jnp.minimum(z, 0.0) - soft, stacked, runs[c]))
            runs[c] = runs[c] + jnp.sum(stay, axis=-1, keepdims=True)
    between = [jnp.dot(job[3], tri, preferred_element_type=F32) for job in jobs]
    outs = [None] * len(qs)
    for (c, v_blk, log_sig, _, run), btw in zip(jobs, between):
        rows = log_sig.shape[0]
        a = jnp.exp(log_sig + (btw[:rows] + btw[rows:]) + run)
        if masks[c] is not None:
            a = jnp.where(masks[c], a, 0.0)
        o = jnp.dot(a.astype(BF16), v_blk, preferred_element_type=F32)
        outs[c] = o if outs[c] is None else outs[c] + o
    return outs, runs


def _head_rms(o, gain, first):
    sq = o * o
    s0 = jnp.sum(jnp.where(first, sq, 0.0), axis=-1, keepdims=True)
    s1 = jnp.sum(jnp.where(first, 0.0, sq), axis=-1, keepdims=True)
    ms = jnp.where(first, s0, s1) * (1.0 / HEAD_DIM)
    return o * lax.rsqrt(ms + NORM_EPS) * gain


def _sb_prompt_kernel(q_ref, k_ref, v_ref, tri_ref, gain_ref, o_ref, acc_ref, run_ref, *, blk):
    qi = pl.program_id(2)
    q = q_ref[0] * (HEAD_DIM ** -0.5)
    first = lax.broadcasted_iota(jnp.int32, (1, LANES), 1) < HEAD_DIM
    q_heads = (jnp.where(first, q, 0.0).astype(BF16), jnp.where(first, 0.0, q).astype(BF16))
    tri = tri_ref[...]
    causal = (lax.broadcasted_iota(jnp.int32, (blk, blk), 1)
              < lax.broadcasted_iota(jnp.int32, (blk, blk), 0))

    acc_ref[...] = jnp.zeros_like(acc_ref)
    run_ref[...] = jnp.zeros_like(run_ref)

    sub = blk // ATTN_SPLIT
    chains = [(h, slice(r * sub, (r + 1) * sub)) for h in range(2) for r in range(ATTN_SPLIT)]

    def sweep(kb, n_blocks, mask):
        kv = []
        for j in range(n_blocks):
            start = pl.multiple_of((kb - j) * blk, blk)
            kv.append((k_ref[0, pl.ds(start, blk), :].astype(BF16),
                       v_ref[0, pl.ds(start, blk), :].astype(BF16)))
        outs, runs = _sb_sweep([q_heads[h][rs] for h, rs in chains], kv, tri,
                               [run_ref[h, rs] for h, rs in chains],
                               [None if mask is None else mask[rs] for h, rs in chains])
        for (h, rs), o, run in zip(chains, outs, runs):
            run_ref[h, rs] = run
            acc_ref[h, rs] += o

    sweep(qi, 1, causal)
    rest = qi % ATTN_UNROLL

    def tail(i, carry):
        sweep(qi - 1 - i, 1, None)
        return carry

    lax.fori_loop(0, rest, tail, 0)

    def body(i, carry):
        sweep(qi - 1 - rest - i * ATTN_UNROLL, ATTN_UNROLL, None)
        return carry

    lax.fori_loop(0, qi // ATTN_UNROLL, body, 0)
    o = jnp.where(first, acc_ref[0], acc_ref[1])
    o_ref[0] = _head_rms(o, gain_ref[0], first).astype(o_ref.dtype)


def _sb_prompt(p_sb, gain, tri):
    b, t, w = p_sb.shape
    d_sb = w // 3
    pairs = d_sb // LANES
    blk = ATTN_BLOCK
    assert t % blk == 0
    return pl.pallas_call(
        functools.partial(_sb_prompt_kernel, blk=blk),
        grid=(b, pairs, t // blk),
        in_specs=[
            pl.BlockSpec((1, blk, LANES), lambda bi, p, qi: (bi, qi, p)),
            pl.BlockSpec((1, t, LANES), lambda bi, p, qi: (bi, 0, pairs + p)),
            pl.BlockSpec((1, t, LANES), lambda bi, p, qi: (bi, 0, 2 * pairs + p)),
            pl.BlockSpec((blk, blk), lambda bi, p, qi: (0, 0)),
            pl.BlockSpec((1, 1, LANES), lambda bi, p, qi: (p, 0, 0)),
        ],
        out_specs=pl.BlockSpec((1, blk, LANES), lambda bi, p, qi: (bi, qi, p)),
        out_shape=jax.ShapeDtypeStruct((b, t, d_sb), BF16),
        scratch_shapes=[pltpu.VMEM((2, blk, LANES), F32), pltpu.VMEM((2, blk, 1), F32)],
        compiler_params=_params("parallel", "parallel", "arbitrary"),
        name="sb_prompt",
    )(p_sb, p_sb, p_sb, tri, gain)


def _sb_sample_kernel(q_ref, kn_ref, vn_ref, kc_ref, vc_ref, tri_ref, gain_ref, o_ref,
                      qs_ref, acc_ref, run_ref, *, n_q, n_heads):
    j = pl.program_id(1)
    rows = n_heads * n_q
    d_sb = n_heads * HEAD_DIM
    row_head = lax.broadcasted_iota(jnp.int32, (rows, 1), 0) >> _log2(n_q)
    tri = tri_ref[...]

    @pl.when(j == 0)
    def _():
        lane_head = lax.broadcasted_iota(jnp.int32, (1, d_sb), 1) >> _log2(HEAD_DIM)
        q = jnp.tile(q_ref[0] * (HEAD_DIM ** -0.5), (n_heads, 1))
        qs = jnp.where(row_head == lane_head, q, 0.0).astype(BF16)
        qs_ref[...] = qs
        n_new = kn_ref.shape[1]
        q_pos = lax.broadcasted_iota(jnp.int32, (rows, 1), 0) & (n_q - 1)
        mask = lax.broadcasted_iota(jnp.int32, (1, n_new), 1) < q_pos
        outs, runs = _sb_sweep([qs], [(kn_ref[0].astype(BF16), vn_ref[0].astype(BF16))],
                               tri_ref[:n_new, :n_new], [jnp.zeros((rows, 1), F32)], [mask])
        run_ref[...] = runs[0]
        acc_ref[...] = outs[0]

    half = rows // 2
    chains = [slice(0, half), slice(half, rows)]
    outs, runs = _sb_sweep([qs_ref[rs, :] for rs in chains],
                           [(kc_ref[0].astype(BF16), vc_ref[0].astype(BF16))], tri,
                           [run_ref[rs, :] for rs in chains], [None, None])
    for rs, o, run in zip(chains, outs, runs):
        run_ref[rs, :] = run
        acc_ref[rs, :] += o

    @pl.when(j == pl.num_programs(1) - 1)
    def _():
        lane_head = lax.broadcasted_iota(jnp.int32, (1, d_sb), 1) >> _log2(HEAD_DIM)
        o = jnp.zeros((n_q, d_sb), F32)
        for h in range(n_heads):
            o = o + jnp.where(lane_head == h, acc_ref[h * n_q:(h + 1) * n_q, :], 0.0)
        first = (lax.broadcasted_iota(jnp.int32, (1, LANES), 1) < HEAD_DIM)
        for p in range(d_sb // LANES):
            sl = slice(p * LANES, (p + 1) * LANES)
            o_ref[0, :, sl] = _head_rms(o[:, sl], gain_ref[:, sl], first).astype(o_ref.dtype)


def _sb_sample(p_sb, cache_k, cache_v, gain, tri):
    b, n_q, w = p_sb.shape
    d_sb = w // 3
    n_heads = d_sb // HEAD_DIM
    past = cache_k.shape[1]
    blk = tri.shape[0]
    assert past % blk == 0 and n_q <= LANES
    n_blocks = past // blk
    pad = ((0, 0), (0, LANES - n_q), (0, 0))
    k_new = jnp.pad(p_sb[:, :, d_sb:2 * d_sb], pad)
    v_new = jnp.pad(p_sb[:, :, 2 * d_sb:], pad)
    rows = n_heads * n_q
    return pl.pallas_call(
        functools.partial(_sb_sample_kernel, n_q=n_q, n_heads=n_heads),
        grid=(b, n_blocks),
        in_specs=[
            pl.BlockSpec((1, n_q, d_sb), lambda bi, j: (bi, 0, 0)),
            pl.BlockSpec((1, LANES, d_sb), lambda bi, j: (bi, 0, 0)),
            pl.BlockSpec((1, LANES, d_sb), lambda bi, j: (bi, 0, 0)),
            pl.BlockSpec((1, blk, d_sb), lambda bi, j: (bi, n_blocks - 1 - j, 0)),
            pl.BlockSpec((1, blk, d_sb), lambda bi, j: (bi, n_blocks - 1 - j, 0)),
            pl.BlockSpec((blk, blk), lambda bi, j: (0, 0)),
            pl.BlockSpec((1, d_sb), lambda bi, j: (0, 0)),
        ],
        out_specs=pl.BlockSpec((1, n_q, d_sb), lambda bi, j: (bi, 0, 0)),
        out_shape=jax.ShapeDtypeStruct((b, n_q, d_sb), BF16),
        scratch_shapes=[pltpu.VMEM((rows, d_sb), BF16), pltpu.VMEM((rows, d_sb), F32),
                        pltpu.VMEM((rows, 1), F32)],
        compiler_params=_params("parallel", "arbitrary"),
        name="sb_sample",
    )(p_sb, k_new, v_new, cache_k, cache_v, tri, gain)


def _rwkv_prep_kernel(p_ref, tail_ref, shift_ref, mu_ref, w0_ref, w2_ref, a0_ref, a2_ref, g2_ref,
                      r_ref, k_ref, v_ref, lw_ref, al_ref, g_ref, *, d_rw, wd, wa):
    i = pl.program_id(1)
    p = p_ref[0]
    tm = p.shape[0]
    before = jnp.where(i == 0, shift_ref[0], tail_ref[0, 7:8, :])
    row = lax.broadcasted_iota(jnp.int32, (tm, 1), 0)
    prev = jnp.where(row == 0, before, pltpu.roll(p, 1, 0))
    xs = p + (prev - p) * mu_ref[...]
    o1, o2, o3 = d_rw, 2 * d_rw, 3 * d_rw
    r_ref[0] = xs[:, :o1]
    k_ref[0] = xs[:, o1:o2]
    v_ref[0] = xs[:, o2:o3]
    dw = xs[:, o3:o3 + wd]
    da = xs[:, o3 + wd:o3 + wd + wa]
    dg = xs[:, o3 + wd + wa:]
    w_pre = w0_ref[...] + _dg(jnp.tanh(dw), w2_ref[...], _NN)
    w_log = -(jnp.maximum(-w_pre, 0.0) + jnp.log1p(jnp.exp(-jnp.abs(w_pre)))) - 0.5
    lw_ref[0] = -jnp.exp(w_log)
    al_ref[0] = jax.nn.sigmoid(a0_ref[...] + _dg(da, a2_ref[...], _NN))
    g_ref[0] = _dg(jax.nn.sigmoid(dg), g2_ref[...], _NN)


def _rwkv_prep(p_rw, shift0, mu, w0, w2, a0, a2, g2):
    b, t, pw = p_rw.shape
    d_rw = w0.shape[1]
    wd, wa = w2.shape[0], a2.shape[0]
    tm = _tile(t, ROW_TILE // 2)
    tail = max(tm // 8, 1)
    out = jax.ShapeDtypeStruct((b, t, d_rw), F32)
    row_spec = pl.BlockSpec((1, tm, d_rw), lambda bi, i: (bi, i, 0))
    full = lambda a: pl.BlockSpec(a.shape, lambda bi, i: (0,) * a.ndim)
    return pl.pallas_call(
        functools.partial(_rwkv_prep_kernel, d_rw=d_rw, wd=wd, wa=wa),
        grid=(b, t // tm),
        in_specs=[
            pl.BlockSpec((1, tm, pw), lambda bi, i: (bi, i, 0)),
            pl.BlockSpec((1, 8, pw), lambda bi, i: (bi, jnp.maximum(i * tail - 1, 0), 0)),
            pl.BlockSpec((1, 1, pw), lambda bi, i: (bi, 0, 0)),
            full(mu), full(w0), full(w2), full(a0), full(a2), full(g2),
        ],
        out_specs=[row_spec] * 6,
        out_shape=[out] * 6,
        compiler_params=_params("parallel", "parallel"),
        name="rwkv_prep",
    )(p_rw, p_rw, shift0, mu, w0, w2, a0, a2, g2)


def _stack(x, first):
    return jnp.concatenate([jnp.where(first, x, 0.0), jnp.where(first, 0.0, x)], axis=0)


def _each(f, *cols):
    return [f(*xs) for xs in zip(*cols)]


def _sum_sel(x, sel, pieces):
    parts = _split2(x) if pieces == 2 else _split3(x)
    n = x.shape[0]
    out = _dg(jnp.concatenate(parts, axis=0), sel, _NN)
    return sum(out[i * n:(i + 1) * n] for i in range(1, len(parts))) + out[:n]


def _unit_lower_inverses(lows, c):
    n = lows[0].shape[0]
    r = lax.broadcasted_iota(jnp.int32, (n, n), 0)
    col = lax.broadcasted_iota(jnp.int32, (n, n), 1)
    eye = jnp.where(r == col, 1.0, 0.0)
    invs = [eye for _ in lows]
    for lvl in range(_log2(c)):
        same = (r >> (lvl + 1)) == (col >> (lvl + 1))
        lower_left = same & (((r >> lvl) & 1) == 1) & (((col >> lvl) & 1) == 0)
        offs = [jnp.where(lower_left, low, 0.0) for low in lows]
        if lvl == 0:
            invs = [eye + off for off in offs]
            continue
        left = _each(lambda inv, off: _dg(inv, off, _NN), invs, offs)
        invs = _each(lambda inv, x: inv + _dg(x, inv, _NN), invs, left)
    return invs


def _rwkv_scan_kernel(r_ref, k_ref, v_ref, lw_ref, al_ref, g_ref, kk_ref, ka_ref, rk_ref,
                      lnw_ref, lnb_ref, s0_ref, o_ref, sT_ref, s_ref, *, c, t_valid, group):
    ci = pl.program_id(2)

    @pl.when(ci == 0)
    def _():
        s_ref[...] = s0_ref[0]

    lanes = [slice(p * LANES, (p + 1) * LANES) for p in range(group)]
    take = lambda ref: [ref[0, :, sl] for sl in lanes]
    par = lambda ref: [ref[:, sl] for sl in lanes]
    n = 2 * c
    first = lax.broadcasted_iota(jnp.int32, (1, LANES), 1) < HEAD_DIM
    valid = (ci * c + lax.broadcasted_iota(jnp.int32, (c, 1), 0)) < t_valid
    pair_sum = _head_pair_sum_matrix()
    pair_mean = _head_pair_sum_matrix(1.0 / HEAD_DIM)
    t_i = lax.broadcasted_iota(jnp.int32, (c, c), 0)
    s_i = lax.broadcasted_iota(jnp.int32, (c, c), 1)
    upto = jnp.where(s_i <= t_i, 1.0, 0.0).astype(BF16)
    tt = lax.broadcasted_iota(jnp.int32, (n, n), 0) & (c - 1)
    ss = lax.broadcasted_iota(jnp.int32, (n, n), 1) & (c - 1)
    strict, incl = ss < tt, ss <= tt
    diag = (lax.broadcasted_iota(jnp.int32, (LANES, LANES), 0)
            == lax.broadcasted_iota(jnp.int32, (LANES, LANES), 1))

    r = take(r_ref)
    k = [jnp.where(valid, x, 0.0) for x in take(k_ref)]
    v = [jnp.where(valid, x, 0.0) for x in take(v_ref)]
    lw = [jnp.where(valid, x, 0.0) for x in take(lw_ref)]
    al = take(al_ref)

    kk = _each(lambda x, w: x * w, k, par(kk_ref))
    norm = [jnp.sqrt(_sum_sel(x * x, pair_sum, 2)) for x in kk]
    kk = _each(lambda x, nr: x / jnp.maximum(nr, KK_EPS), kk, norm)
    k_mod = _each(lambda x, a_, w: x * (1.0 + (a_ - 1.0) * w), k, al, par(ka_ref))
    b = _each(lambda x, a_: x * a_, kk, al)

    def cumsum(x):
        out = _dg(upto, jnp.concatenate(_split3(x), axis=1), _NN)
        return out[:, :LANES] + out[:, LANES:2 * LANES] + out[:, 2 * LANES:]

    cum = [cumsum(x) for x in lw]
    cum_end = [x[c - 1:c, :] for x in cum]
    dec_out = [jnp.exp(-x) for x in cum]
    dec_rest = _each(lambda e, x: jnp.exp(e - x), cum_end, cum)
    a_t = _each(lambda x, cu, l: _stack(-x * jnp.exp(cu - l), first), kk, cum, lw)
    r_t = _each(lambda x, cu: _stack(x * jnp.exp(cu), first), r, cum)
    b_t = _each(lambda x, d: _stack(x * d, first), b, dec_out)
    k_t = _each(lambda x, d: _stack(x * d, first), k_mod, dec_out)
    b_h = _each(lambda x, d: _stack(x * d, first), b, dec_rest)
    k_h = _each(lambda x, d: _stack(x * d, first), k_mod, dec_rest)
    v_s = [_stack(x, first) for x in v]

    prod = _each(lambda a_, r_, b_, k_: _dg(jnp.concatenate([a_, r_], axis=0),
                                            jnp.concatenate([b_, k_], axis=0), _NT), a_t, r_t, b_t, k_t)
    l_ab = [jnp.where(strict, x[:n, :n], 0.0) for x in prod]
    l_ak = [jnp.where(strict, x[:n, n:], 0.0) for x in prod]
    m_r = [jnp.where(jnp.concatenate([incl, incl], axis=1), x[n:], 0.0) for x in prod]

    solve = _unit_lower_inverses(l_ab, c)
    lv = _each(lambda l_, v_: _dg(l_, v_, _NN), l_ak, v_s)
    au = _each(lambda t_, a_, x: _dg(t_, jnp.concatenate([a_, x], axis=1), _NN), solve, a_t, lv)
    rhs = _each(lambda x, v_: jnp.concatenate(
        [x, jnp.concatenate([jnp.zeros_like(v_), v_], axis=1)], axis=0), au, v_s)
    ry = _each(lambda m, x: _dg(m, x, _NN), m_r, rhs)
    gh = _each(lambda b_, k_, x: _dg(jnp.concatenate([b_, k_], axis=0), x, _TN), b_h, k_h, rhs)

    lhs = _each(lambda r_, x, y, e: jnp.concatenate(
        [r_ + x[:, :LANES], jnp.where(diag, jnp.exp(e), 0.0) + y[:, :LANES]], axis=0), r_t, ry, gh, cum_end)
    step = [_dg3(x, s_ref[p]) for p, x in enumerate(lhs)]
    for p in range(group):
        s_ref[p] = step[p][n:] + gh[p][:, LANES:]
    y = _each(lambda st, x: st[:c] + st[c:n] + x[:c, LANES:] + x[c:, LANES:], step, ry)

    mean = [_sum_sel(x, pair_mean, 2) for x in y]
    d = _each(lambda x, m: x - m, y, mean)
    var = [_sum_sel(x * x, pair_mean, 2) for x in d]
    bonus = _each(lambda r_, k_, w: _sum_sel(r_ * k_ * w, pair_sum, 2), r, k_mod, par(rk_ref))
    for p, sl in enumerate(lanes):
        yn = d[p] * lax.rsqrt(var[p] + GN_EPS) * lnw_ref[:, sl] + lnb_ref[:, sl]
        o_ref[0, :, sl] = ((yn + bonus[p] * v[p]) * g_ref[0, :, sl]).astype(o_ref.dtype)

    @pl.when(ci == pl.num_programs(2) - 1)
    def _():
        sT_ref[0] = s_ref[...]


def _rwkv_scan(r, k, v, lw, al, g, k_k, k_a, r_k, ln_w, ln_b, s0, t_valid):
    b, t_in, d_rw = r.shape
    pairs = d_rw // LANES
    group = SCAN_GROUP if pairs % SCAN_GROUP == 0 else 1
    c = SCAN_CHUNK
    t = -(-t_in // c) * c
    if t != t_in:
        r, k, v, lw, al, g = (jnp.pad(x, ((0, 0), (0, t - t_in), (0, 0))) for x in (r, k, v, lw, al, g))
    width = group * LANES
    row_spec = pl.BlockSpec((1, c, width), lambda bi, p, ci: (bi, ci, p))
    par_spec = pl.BlockSpec((1, width), lambda bi, p, ci: (0, p))
    st_spec = pl.BlockSpec((1, group, LANES, LANES), lambda bi, p, ci: (bi, p, 0, 0))
    o, s_t = pl.pallas_call(
        functools.partial(_rwkv_scan_kernel, c=c, t_valid=t_valid, group=group),
        grid=(b, pairs // group, t // c),
        in_specs=[row_spec] * 6 + [par_spec] * 5 + [st_spec],
        out_specs=[row_spec, st_spec],
        out_shape=[jax.ShapeDtypeStruct((b, t, d_rw), BF16),
                   jax.ShapeDtypeStruct((b, pairs, LANES, LANES), F32)],
        scratch_shapes=[pltpu.VMEM((group, LANES, LANES), F32)],
        compiler_params=_params("parallel", "parallel", "arbitrary"),
        name="rwkv_scan",
    )(r, k, v, lw, al, g, k_k, k_a, r_k, ln_w, ln_b, s0)
    return o[:, :t_in], s_t


def _state_to_pairs(s):
    b, h, e, _ = s.shape
    st = jnp.swapaxes(s, -1, -2).reshape(b, h // 2, 2, e, e)
    z = jnp.zeros_like(st[:, :, 0])
    top = jnp.concatenate([st[:, :, 0], z], axis=-1)
    bot = jnp.concatenate([z, st[:, :, 1]], axis=-1)
    return jnp.concatenate([top, bot], axis=-2)


def _pairs_to_state(sp):
    b, pairs = sp.shape[:2]
    e = HEAD_DIM
    blocks = jnp.stack([sp[:, :, :e, :e], sp[:, :, e:, e:]], axis=2)
    return jnp.swapaxes(blocks, -1, -2).reshape(b, 2 * pairs, e, e)


def _pad_cols(x, width):
    return jnp.pad(x, [(0, 0)] * (x.ndim - 1) + [(0, width - x.shape[-1])])


def _rw_layout(x, d_rw, ranks, widths):
    parts = [x[..., :3 * d_rw]]
    o = 3 * d_rw
    for rank, width in zip(ranks, widths):
        parts.append(_pad_cols(x[..., o:o + rank], width))
        o += rank
    return jnp.concatenate(parts, axis=-1)


def _rw_unlayout(x, d_rw, ranks, widths):
    parts = [x[..., :3 * d_rw]]
    o = 3 * d_rw
    for rank, width in zip(ranks, widths):
        parts.append(x[..., o:o + rank])
        o += width
    return jnp.concatenate(parts, axis=-1)


def _layer(h, lw, cache, s0_pairs, shift0, t_valid, tri):
    b, t, d = h.shape
    d_sb = lw["sb_gain"].shape[-1]
    x = h.reshape(b * t, d)
    x = _ffn(x, lw["ffn1_pre"], lw["ffn1_post"], lw["ffn1_gate"], lw["ffn1_up"], lw["ffn1_down"])
    p_sb = _norm_matmul(x, lw["mix_pre"], lw["w_in_sb"], 1024).reshape(b, t, -1)
    p_rw = _norm_matmul(x, lw["mix_pre"], lw["w_in_rw"], 1280).reshape(b, t, -1)
    if cache is None:
        o_sb = _sb_prompt(p_sb, lw["sb_gain_pairs"], tri)
    else:
        o_sb = _sb_sample(p_sb, cache[0], cache[1], lw["sb_gain"], tri)
    r, k, v, lwd, al, g = _rwkv_prep(p_rw, shift0, lw["mu"], lw["w0"], lw["w2"], lw["a0"], lw["a2"],
                                     lw["g2"])
    o_rw, s_t = _rwkv_scan(r, k, v, lwd, al, g, lw["k_k"], lw["k_a"], lw["r_k"], lw["ln_w"],
                           lw["ln_b"], s0_pairs, t_valid)
    x = _mix_out(o_sb.reshape(b * t, -1), o_rw.reshape(b * t, -1), lw["w_out_sb"], lw["w_out_rw"],
                 x, lw["mix_post"])
    x = _ffn(x, lw["ffn2_pre"], lw["ffn2_post"], lw["ffn2_gate"], lw["ffn2_up"], lw["ffn2_down"])
    k_new = p_sb[:, :t_valid, d_sb:2 * d_sb]
    v_new = p_sb[:, :t_valid, 2 * d_sb:]
    return x.reshape(b, t, d), k_new, v_new, s_t, p_rw[:, t_valid - 1]


def kernel(x_prompt, x_sample, cache_sb_k, cache_sb_v, state_rwkv_S, state_rwkv_shift, meta_tokens, ffn1_norm_pre, ffn1_norm_post, ffn1_w_gate, ffn1_w_up, ffn1_w_down, mix_norm_pre, mix_norm_post, w_in, sb_out_gain, rwkv_mu, rwkv_w0, rwkv_w2, rwkv_a0, rwkv_a2, rwkv_g2, rwkv_k_k, rwkv_k_a, rwkv_r_k, rwkv_ln_w, rwkv_ln_b, w_out, ffn2_norm_pre, ffn2_norm_post, ffn2_w_gate, ffn2_w_up, ffn2_w_down):
    depth, d = ffn1_norm_pre.shape
    bp, seq, _ = x_prompt.shape
    bs, dec_seq, _ = x_sample.shape
    n_meta = meta_tokens.shape[0]
    h_sb = sb_out_gain.shape[1]
    d_sb = h_sb * HEAD_DIM
    d_rw = rwkv_w0.shape[1]
    h_rw = d_rw // HEAD_DIM
    ranks = (rwkv_w2.shape[1], rwkv_a2.shape[1], rwkv_g2.shape[1])
    widths = tuple(-(-r // LANES) * LANES for r in ranks)
    p_rw_cols = 3 * d_rw + sum(ranks)
    past = cache_sb_k.shape[2]

    t_valid = n_meta + seq
    t_pad = -(-t_valid // ATTN_BLOCK) * ATTN_BLOCK
    meta = jnp.broadcast_to(meta_tokens[None].astype(x_prompt.dtype), (bp, n_meta, d))
    hp = jnp.concatenate([meta, x_prompt, jnp.zeros((bp, t_pad - t_valid, d), x_prompt.dtype)], axis=1)
    hs = x_sample

    tri_p = _strict_upper(ATTN_BLOCK)
    tri_s = _strict_upper(min(CACHE_BLOCK, past))
    row = lambda x: x.reshape(1, -1)
    pad_rows = lambda w, width: jnp.pad(w, ((0, width - w.shape[0]), (0, 0)))
    w_in_rw_all = _rw_layout(w_in[:, :, 3 * d_sb:], d_rw, ranks, widths).astype(BF16)
    w_in_sb_all = w_in[:, :, :3 * d_sb].astype(BF16)
    ffn_w = [_to_bf16(w) for w in (ffn1_w_gate, ffn1_w_up, ffn1_w_down, ffn2_w_gate, ffn2_w_up, ffn2_w_down)]
    mu_all = _rw_layout(rwkv_mu, d_rw, ranks, widths)

    outs = [[] for _ in range(8)]
    s0_p = jnp.zeros((bp, h_rw // 2, LANES, LANES), F32)
    shift0_p = jnp.zeros((bp, 1, w_in_rw_all.shape[-1]), F32)
    for l in range(depth):
        lw = dict(
            ffn1_pre=row(ffn1_norm_pre[l]), ffn1_post=row(ffn1_norm_post[l]),
            ffn1_gate=ffn_w[0][l], ffn1_up=ffn_w[1][l], ffn1_down=ffn_w[2][l],
            ffn2_pre=row(ffn2_norm_pre[l]), ffn2_post=row(ffn2_norm_post[l]),
            ffn2_gate=ffn_w[3][l], ffn2_up=ffn_w[4][l], ffn2_down=ffn_w[5][l],
            mix_pre=row(mix_norm_pre[l]), mix_post=row(mix_norm_post[l]),
            w_in_sb=w_in_sb_all[l], w_in_rw=w_in_rw_all[l],
            sb_gain=row(sb_out_gain[l]), sb_gain_pairs=sb_out_gain[l].reshape(h_sb // 2, 1, LANES),
            mu=row(mu_all[l]), w0=row(rwkv_w0[l]), a0=row(rwkv_a0[l]),
            w2=pad_rows(rwkv_w2[l], widths[0]).astype(BF16),
            a2=pad_rows(rwkv_a2[l], widths[1]).astype(BF16),
            g2=pad_rows(rwkv_g2[l], widths[2]).astype(BF16),
            k_k=row(rwkv_k_k[l]), k_a=row(rwkv_k_a[l]), r_k=row(rwkv_r_k[l]),
            ln_w=row(rwkv_ln_w[l]), ln_b=row(rwkv_ln_b[l]),
            w_out_sb=w_out[l, :d_sb].astype(BF16), w_out_rw=w_out[l, d_sb:].astype(BF16),
        )
        hp, kp, vp, sp, shp = _layer(hp, lw, None, s0_p, shift0_p, t_valid, tri_p)
        cache = (cache_sb_k[l].reshape(bs, past, d_sb), cache_sb_v[l].reshape(bs, past, d_sb))
        shift0_s = _rw_layout(state_rwkv_shift[l], d_rw, ranks, widths)[:, None, :]
        hs, ks, vs, ss, shs = _layer(hs, lw, cache, _state_to_pairs(state_rwkv_S[l].astype(F32)),
                                     shift0_s, dec_seq, tri_s)
        per_layer = (
            kp.reshape(bp, t_valid, h_sb, HEAD_DIM), vp.reshape(bp, t_valid, h_sb, HEAD_DIM),
            _pairs_to_state(sp).astype(state_rwkv_S.dtype),
            _rw_unlayout(shp, d_rw, ranks, widths).astype(state_rwkv_shift.dtype),
            ks.reshape(bs, dec_seq, h_sb, HEAD_DIM), vs.reshape(bs, dec_seq, h_sb, HEAD_DIM),
            _pairs_to_state(ss).astype(state_rwkv_S.dtype),
            _rw_unlayout(shs, d_rw, ranks, widths).astype(state_rwkv_shift.dtype),
        )
        for acc, val in zip(outs, per_layer):
            acc.append(val)
    assert p_rw_cols == state_rwkv_shift.shape[-1]
    y_prompt = hp[:, n_meta:t_valid]
    return (y_prompt, hs) + tuple(jnp.stack(o) for o in outs)
```

```python
import functools

import jax
import jax.numpy as jnp
from jax import lax
from jax.experimental import pallas as pl
from jax.experimental.pallas import tpu as pltpu

F32 = jnp.float32
BF16 = jnp.bfloat16

HEAD_DIM = 64
LANES = 128
NORM_EPS = 1e-6
GN_EPS = 64e-5
KK_EPS = 1e-12
VMEM_LIMIT_BYTES = 56 * 2 ** 20

LOG2E = 1.4426950408889634
ATTN_BLOCK = 256
ATTN_SPLIT = 1
ATTN_UNROLL = 4
CACHE_BLOCK = 512
SCAN_CHUNK = 64
SCAN_GROUP = 8
ROW_TILE = 512
FF_TILE = 512
CAST_BLOCK_BYTES = 8 * 2 ** 20


def _params(*sem):
    return pltpu.CompilerParams(dimension_semantics=sem, vmem_limit_bytes=VMEM_LIMIT_BYTES)


def _tile(n, pref, mult=16):
    t = min(pref, n)
    t -= t % mult
    while t > mult and n % t:
        t -= mult
    assert t >= mult and n % t == 0, (n, pref)
    return t


def _log2(n):
    k = n.bit_length() - 1
    assert 1 << k == n, n
    return k


_NN = (((1,), (0,)), ((), ()))
_NT = (((1,), (1,)), ((), ()))
_TN = (((0,), (0,)), ((), ()))


def _dg(a, b, dims):
    return lax.dot_general(a.astype(BF16), b.astype(BF16), dims, preferred_element_type=F32)


def _split2(x):
    hi = x.astype(BF16)
    lo = (x - hi.astype(F32)).astype(BF16)
    return hi, lo


def _split3(x):
    hi = x.astype(BF16)
    r = x - hi.astype(F32)
    mid = r.astype(BF16)
    lo = (r - mid.astype(F32)).astype(BF16)
    return hi, mid, lo


def _dg3(a, b, dims=_NN):
    ah, al = _split2(a)
    bh, bl = _split2(b)
    return _dg(ah, bh, dims) + (_dg(ah, bl, dims) + _dg(al, bh, dims))


def _dg_sel(x, sel, pieces):
    parts = _split2(x) if pieces == 2 else _split3(x)
    out = _dg(parts[0], sel, _NN)
    for p in parts[1:]:
        out = out + _dg(p, sel, _NN)
    return out


def _sel_dg(sel, x, pieces):
    parts = _split2(x) if pieces == 2 else _split3(x)
    out = _dg(sel, parts[0], _NN)
    for p in parts[1:]:
        out = out + _dg(sel, p, _NN)
    return out


def _rms(x, eps=NORM_EPS):
    return x * lax.rsqrt(jnp.mean(x * x, axis=-1, keepdims=True) + eps)


def _head_pair_sum_matrix(scale=1.0):
    r = lax.broadcasted_iota(jnp.int32, (LANES, LANES), 0) >> _log2(HEAD_DIM)
    c = lax.broadcasted_iota(jnp.int32, (LANES, LANES), 1) >> _log2(HEAD_DIM)
    return jnp.where(r == c, scale, 0.0).astype(BF16)


def _cast_kernel(x_ref, o_ref):
    o_ref[...] = x_ref[...].astype(o_ref.dtype)


def _to_bf16(x):
    flat = x.reshape(-1, x.shape[-1])
    n, w = flat.shape
    tm = _tile(n, max(16, CAST_BLOCK_BYTES // (4 * w)))
    out = pl.pallas_call(
        _cast_kernel,
        grid=(n // tm,),
        in_specs=[pl.BlockSpec((tm, w), lambda i: (i, 0))],
        out_specs=pl.BlockSpec((tm, w), lambda i: (i, 0)),
        out_shape=jax.ShapeDtypeStruct((n, w), BF16),
        compiler_params=_params("parallel"),
        name="to_bf16",
    )(flat)
    return out.reshape(x.shape)


def _ffn_kernel(x_ref, gpre_ref, gpost_ref, wg_ref, wu_ref, wd_ref, o_ref, u_ref, acc_ref):
    f = pl.program_id(1)

    @pl.when(f == 0)
    def _():
        u_ref[...] = (_rms(x_ref[...]) * gpre_ref[...]).astype(BF16)
        acc_ref[...] = jnp.zeros_like(acc_ref)

    u = u_ref[...]
    gate = jnp.dot(u, wg_ref[...], preferred_element_type=F32)
    up = jnp.dot(u, wu_ref[...], preferred_element_type=F32)
    act = (gate * jax.nn.sigmoid(gate) * up).astype(BF16)
    acc_ref[...] += jnp.dot(act, wd_ref[...], preferred_element_type=F32)

    @pl.when(f == pl.num_programs(1) - 1)
    def _():
        o_ref[...] = x_ref[...] + 0.5 * (_rms(acc_ref[...]) * gpost_ref[...])


def _ffn(x, g_pre, g_post, w_gate, w_up, w_down):
    n, d = x.shape
    d_ff = w_gate.shape[1]
    tm = _tile(n, ROW_TILE)
    tf = _tile(d_ff, FF_TILE, LANES)
    return pl.pallas_call(
        _ffn_kernel,
        grid=(n // tm, d_ff // tf),
        in_specs=[
            pl.BlockSpec((tm, d), lambda i, f: (i, 0)),
            pl.BlockSpec((1, d), lambda i, f: (0, 0)),
            pl.BlockSpec((1, d), lambda i, f: (0, 0)),
            pl.BlockSpec((d, tf), lambda i, f: (0, f)),
            pl.BlockSpec((d, tf), lambda i, f: (0, f)),
            pl.BlockSpec((tf, d), lambda i, f: (f, 0)),
        ],
        out_specs=pl.BlockSpec((tm, d), lambda i, f: (i, 0)),
        out_shape=jax.ShapeDtypeStruct((n, d), F32),
        scratch_shapes=[pltpu.VMEM((tm, d), BF16), pltpu.VMEM((tm, d), F32)],
        compiler_params=_params("parallel", "arbitrary"),
        name="ffn_half",
    )(x, g_pre, g_post, w_gate, w_up, w_down)


def _norm_matmul_kernel(x_ref, g_ref, w_ref, o_ref):
    u = (_rms(x_ref[...]) * g_ref[...]).astype(BF16)
    o_ref[...] = jnp.dot(u, w_ref[...], preferred_element_type=F32)


def _norm_matmul(x, g, w, col_tile):
    n, d = x.shape
    cols = w.shape[1]
    tm = _tile(n, ROW_TILE)
    tn = _tile(cols, col_tile, LANES)
    return pl.pallas_call(
        _norm_matmul_kernel,
        grid=(cols // tn, n // tm),
        in_specs=[
            pl.BlockSpec((tm, d), lambda j, i: (i, 0)),
            pl.BlockSpec((1, d), lambda j, i: (0, 0)),
            pl.BlockSpec((d, tn), lambda j, i: (0, j)),
        ],
        out_specs=pl.BlockSpec((tm, tn), lambda j, i: (i, j)),
        out_shape=jax.ShapeDtypeStruct((n, cols), F32),
        compiler_params=_params("parallel", "parallel"),
        name="mix_in",
    )(x, g, w)


def _mix_out_kernel(osb_ref, orw_ref, wsb_ref, wrw_ref, h_ref, g_ref, o_ref):
    o = jnp.dot(osb_ref[...], wsb_ref[...], preferred_element_type=F32)
    o = o + jnp.dot(orw_ref[...], wrw_ref[...], preferred_element_type=F32)
    o_ref[...] = h_ref[...] + _rms(o) * g_ref[...]


def _mix_out(o_sb, o_rw, w_sb, w_rw, h, g):
    n, d = h.shape
    tm = _tile(n, ROW_TILE // 2)
    return pl.pallas_call(
        _mix_out_kernel,
        grid=(n // tm,),
        in_specs=[
            pl.BlockSpec((tm, o_sb.shape[1]), lambda i: (i, 0)),
            pl.BlockSpec((tm, o_rw.shape[1]), lambda i: (i, 0)),
            pl.BlockSpec(w_sb.shape, lambda i: (0, 0)),
            pl.BlockSpec(w_rw.shape, lambda i: (0, 0)),
            pl.BlockSpec((tm, d), lambda i: (i, 0)),
            pl.BlockSpec((1, d), lambda i: (0, 0)),
        ],
        out_specs=pl.BlockSpec((tm, d), lambda i: (i, 0)),
        out_shape=jax.ShapeDtypeStruct((n, d), F32),
        compiler_params=_params("parallel"),
        name="mix_out",
    )(o_sb, o_rw, w_sb, w_rw, h, g)


def _strict_upper(n):
    r = lax.broadcasted_iota(jnp.int32, (n, n), 0)
    c = lax.broadcasted_iota(jnp.int32, (n, n), 1)
    return jnp.where(r > c, 1.0, 0.0).astype(BF16)


def _split_top16(x):
    bits = lax.bitcast_convert_type(x, jnp.uint32) & jnp.uint32(0xFFFF0000)
    hi = lax.bitcast_convert_type(bits, F32)
    return hi.astype(BF16), (x - hi).astype(BF16)


def _sb_sweep(qs, kv_blocks, tri, runs, masks, feature_major=False):
    jobs = []
    for k_blk, v_blk in kv_blocks:
        for c, (q, mask) in enumerate(zip(qs, masks)):
            z = _dg(q, k_blk, _NN if feature_major else _NT)
            nz = -z
            soft = jnp.log(1.0 + jnp.exp(jnp.minimum(z, nz)))
            stay = jnp.minimum(nz, 0.0) - soft
            log_sig = stay + z
            if mask is not None:
                stay = jnp.where(mask, stay, 0.0)
            stacked = jnp.concatenate(_split_top16(stay), axis=0)
            jobs.append((c, v_blk, log_sig, stacked, stay[:, :1]))
    between = [jnp.dot(job[3], tri, preferred_element_type=F32) for job in jobs]
    runs = list(runs)
    outs = [None] * len(qs)
    for (c, v_blk, log_sig, _, stay0), btw in zip(jobs, between):
        rows = log_sig.shape[0]
        right = btw[:rows] + btw[rows:]
        a = jnp.exp(log_sig + right)
        if masks[c] is not None:
            a = jnp.where(masks[c], a, 0.0)
        o = jnp.exp(runs[c]) * _dg(a, v_blk, _NT if feature_major else _NN)
        outs[c] = o if outs[c] is None else outs[c] + o
        runs[c] = runs[c] + (right[:, :1] + stay0)
    return outs, runs


def _head_rms(o, gain, first):
    sq = o * o
    s0 = jnp.sum(jnp.where(first, sq, 0.0), axis=-1, keepdims=True)
    s1 = jnp.sum(jnp.where(first, 0.0, sq), axis=-1, keepdims=True)
    ms = jnp.where(first, s0, s1) * (1.0 / HEAD_DIM)
    return o * lax.rsqrt(ms + NORM_EPS) * gain


def _sb_prompt_kernel(q_ref, k_ref, v_ref, tri_ref, gain_ref, o_ref, acc_ref, run_ref, *, blk):
    qi = pl.program_id(2)
    q = q_ref[0] * (HEAD_DIM ** -0.5)
    first = lax.broadcasted_iota(jnp.int32, (1, LANES), 1) < HEAD_DIM
    q_heads = (jnp.where(first, q, 0.0).astype(BF16), jnp.where(first, 0.0, q).astype(BF16))
    tri = tri_ref[...]
    causal = (lax.broadcasted_iota(jnp.int32, (blk, blk), 1)
              < lax.broadcasted_iota(jnp.int32, (blk, blk), 0))

    acc_ref[...] = jnp.zeros_like(acc_ref)
    run_ref[...] = jnp.zeros_like(run_ref)

    sub = blk // ATTN_SPLIT
    chains = [(h, slice(r * sub, (r + 1) * sub)) for h in range(2) for r in range(ATTN_SPLIT)]

    def sweep(kb, n_blocks, mask):
        kv = []
        for j in range(n_blocks):
            start = pl.multiple_of((kb - j) * blk, blk)
            kv.append((k_ref[0, pl.ds(start, blk), :].astype(BF16),
                       v_ref[0, pl.ds(start, blk), :].astype(BF16)))
        outs, runs = _sb_sweep([q_heads[h][rs] for h, rs in chains], kv, tri,
                               [run_ref[h, rs] for h, rs in chains],
                               [None if mask is None else mask[rs] for h, rs in chains])
        for (h, rs), o, run in zip(chains, outs, runs):
            run_ref[h, rs] = run
            acc_ref[h, rs] += o

    sweep(qi, 1, causal)
    rest = qi % ATTN_UNROLL

    def tail(i, carry):
        sweep(qi - 1 - i, 1, None)
        return carry

    lax.fori_loop(0, rest, tail, 0)

    def body(i, carry):
        sweep(qi - 1 - rest - i * ATTN_UNROLL, ATTN_UNROLL, None)
        return carry

    lax.fori_loop(0, qi // ATTN_UNROLL, body, 0)
    o = jnp.where(first, acc_ref[0], acc_ref[1])
    o_ref[0] = _head_rms(o, gain_ref[0], first).astype(o_ref.dtype)


def _sb_prompt(p_sb, gain, tri):
    b, t, w = p_sb.shape
    d_sb = w // 3
    pairs = d_sb // LANES
    blk = ATTN_BLOCK
    assert t % blk == 0
    return pl.pallas_call(
        functools.partial(_sb_prompt_kernel, blk=blk),
        grid=(b, pairs, t // blk),
        in_specs=[
            pl.BlockSpec((1, blk, LANES), lambda bi, p, qi: (bi, qi, p)),
            pl.BlockSpec((1, t, LANES), lambda bi, p, qi: (bi, 0, pairs + p)),
            pl.BlockSpec((1, t, LANES), lambda bi, p, qi: (bi, 0, 2 * pairs + p)),
            pl.BlockSpec((blk, blk), lambda bi, p, qi: (0, 0)),
            pl.BlockSpec((1, 1, LANES), lambda bi, p, qi: (p, 0, 0)),
        ],
        out_specs=pl.BlockSpec((1, blk, LANES), lambda bi, p, qi: (bi, qi, p)),
        out_shape=jax.ShapeDtypeStruct((b, t, d_sb), BF16),
        scratch_shapes=[pltpu.VMEM((2, blk, LANES), F32), pltpu.VMEM((2, blk, 1), F32)],
        compiler_params=_params("parallel", "parallel", "arbitrary"),
        name="sb_prompt",
    )(p_sb, p_sb, p_sb, tri, gain)


def _sb_sample_kernel(q_ref, kn_ref, vn_ref, kc_ref, vc_ref, tri_ref, gain_ref, o_ref,
                      qs_ref, acc_ref, run_ref, *, n_q, n_heads):
    j = pl.program_id(1)
    rows = n_heads * n_q
    d_sb = n_heads * HEAD_DIM
    row_head = lax.broadcasted_iota(jnp.int32, (rows, 1), 0) >> _log2(n_q)
    tri = tri_ref[...]

    @pl.when(j == 0)
    def _():
        lane_head = lax.broadcasted_iota(jnp.int32, (1, d_sb), 1) >> _log2(HEAD_DIM)
        q = jnp.tile(q_ref[0] * (HEAD_DIM ** -0.5), (n_heads, 1))
        qs = jnp.where(row_head == lane_head, q, 0.0).astype(BF16)
        qs_ref[...] = qs
        n_new = kn_ref.shape[1]
        q_pos = lax.broadcasted_iota(jnp.int32, (rows, 1), 0) & (n_q - 1)
        mask = lax.broadcasted_iota(jnp.int32, (1, n_new), 1) < q_pos
        outs, runs = _sb_sweep([qs], [(kn_ref[0].astype(BF16), vn_ref[0].astype(BF16))],
                               tri_ref[:n_new, :n_new], [jnp.zeros((rows, 1), F32)], [mask])
        run_ref[...] = runs[0]
        acc_ref[...] = outs[0]

    half = rows // 2
    chains = [slice(0, half), slice(half, rows)]
    sub = tri.shape[0]
    starts = range(kc_ref.shape[3] - sub, -1, -sub)
    kv = [(kc_ref[0, 0, :, s:s + sub].astype(BF16), vc_ref[0, 0, :, s:s + sub].astype(BF16))
          for s in starts]
    outs, runs = _sb_sweep([qs_ref[rs, :] for rs in chains], kv, tri,
                           [run_ref[rs, :] for rs in chains], [None, None], feature_major=True)
    for rs, o, run in zip(chains, outs, runs):
        run_ref[rs, :] = run
        acc_ref[rs, :] += o

    @pl.when(j == pl.num_programs(1) - 1)
    def _():
        lane_head = lax.broadcasted_iota(jnp.int32, (1, d_sb), 1) >> _log2(HEAD_DIM)
        o = jnp.zeros((n_q, d_sb), F32)
        for h in range(n_heads):
            o = o + jnp.where(lane_head == h, acc_ref[h * n_q:(h + 1) * n_q, :], 0.0)
        first = (lax.broadcasted_iota(jnp.int32, (1, LANES), 1) < HEAD_DIM)
        for p in range(d_sb // LANES):
            sl = slice(p * LANES, (p + 1) * LANES)
            o_ref[0, :, sl] = _head_rms(o[:, sl], gain_ref[:, sl], first).astype(o_ref.dtype)


def _sb_sample(p_sb, cache_k, cache_v, layer, gain, tri):
    b, n_q, w = p_sb.shape
    d_sb = w // 3
    n_heads = d_sb // HEAD_DIM
    past = cache_k.shape[3]
    blk = min(CACHE_BLOCK, past)
    assert past % blk == 0 and blk % tri.shape[0] == 0 and n_q <= LANES
    n_blocks = past // blk
    pad = ((0, 0), (0, LANES - n_q), (0, 0))
    k_new = jnp.pad(p_sb[:, :, d_sb:2 * d_sb], pad)
    v_new = jnp.pad(p_sb[:, :, 2 * d_sb:], pad)
    rows = n_heads * n_q
    return pl.pallas_call(
        functools.partial(_sb_sample_kernel, n_q=n_q, n_heads=n_heads),
        grid=(b, n_blocks),
        in_specs=[
            pl.BlockSpec((1, n_q, d_sb), lambda bi, j: (bi, 0, 0)),
            pl.BlockSpec((1, LANES, d_sb), lambda bi, j: (bi, 0, 0)),
            pl.BlockSpec((1, LANES, d_sb), lambda bi, j: (bi, 0, 0)),
            pl.BlockSpec((1, 1, d_sb, blk), lambda bi, j: (layer, bi, 0, n_blocks - 1 - j)),
            pl.BlockSpec((1, 1, d_sb, blk), lambda bi, j: (layer, bi, 0, n_blocks - 1 - j)),
            pl.BlockSpec(tri.shape, lambda bi, j: (0, 0)),
            pl.BlockSpec((1, d_sb), lambda bi, j: (0, 0)),
        ],
        out_specs=pl.BlockSpec((1, n_q, d_sb), lambda bi, j: (bi, 0, 0)),
        out_shape=jax.ShapeDtypeStruct((b, n_q, d_sb), BF16),
        scratch_shapes=[pltpu.VMEM((rows, d_sb), BF16), pltpu.VMEM((rows, d_sb), F32),
                        pltpu.VMEM((rows, 1), F32)],
        compiler_params=_params("parallel", "arbitrary"),
        name="sb_sample",
    )(p_sb, k_new, v_new, cache_k, cache_v, tri, gain)


def _rwkv_prep_kernel(p_ref, tail_ref, shift_ref, mu_ref, w0_ref, w2_ref, a0_ref, a2_ref, g2_ref,
                      r_ref, k_ref, v_ref, lw_ref, al_ref, g_ref, *, d_rw, wd, wa):
    i = pl.program_id(1)
    p = p_ref[0]
    tm = p.shape[0]
    before = jnp.where(i == 0, shift_ref[0], tail_ref[0, 7:8, :])
    row = lax.broadcasted_iota(jnp.int32, (tm, 1), 0)
    prev = jnp.where(row == 0, before, pltpu.roll(p, 1, 0))
    xs = p + (prev - p) * mu_ref[...]
    o1, o2, o3 = d_rw, 2 * d_rw, 3 * d_rw
    r_ref[0] = xs[:, :o1]
    k_ref[0] = xs[:, o1:o2]
    v_ref[0] = xs[:, o2:o3]
    dw = xs[:, o3:o3 + wd]
    da = xs[:, o3 + wd:o3 + wd + wa]
    dg = xs[:, o3 + wd + wa:]
    w_pre = w0_ref[...] + _dg(jnp.tanh(dw), w2_ref[...], _NN)
    w_log = -(jnp.maximum(-w_pre, 0.0) + jnp.log1p(jnp.exp(-jnp.abs(w_pre)))) - 0.5
    lw_ref[0] = -jnp.exp(w_log)
    al_ref[0] = jax.nn.sigmoid(a0_ref[...] + _dg(da, a2_ref[...], _NN))
    g_ref[0] = _dg(jax.nn.sigmoid(dg), g2_ref[...], _NN)


def _rwkv_prep(p_rw, shift0, mu, w0, w2, a0, a2, g2):
    b, t, pw = p_rw.shape
    d_rw = w0.shape[1]
    wd, wa = w2.shape[0], a2.shape[0]
    tm = _tile(t, ROW_TILE // 2)
    tail = max(tm // 8, 1)
    out = jax.ShapeDtypeStruct((b, t, d_rw), F32)
    row_spec = pl.BlockSpec((1, tm, d_rw), lambda bi, i: (bi, i, 0))
    full = lambda a: pl.BlockSpec(a.shape, lambda bi, i: (0,) * a.ndim)
    return pl.pallas_call(
        functools.partial(_rwkv_prep_kernel, d_rw=d_rw, wd=wd, wa=wa),
        grid=(b, t // tm),
        in_specs=[
            pl.BlockSpec((1, tm, pw), lambda bi, i: (bi, i, 0)),
            pl.BlockSpec((1, 8, pw), lambda bi, i: (bi, jnp.maximum(i * tail - 1, 0), 0)),
            pl.BlockSpec((1, 1, pw), lambda bi, i: (bi, 0, 0)),
            full(mu), full(w0), full(w2), full(a0), full(a2), full(g2),
        ],
        out_specs=[row_spec] * 6,
        out_shape=[out] * 6,
        compiler_params=_params("parallel", "parallel"),
        name="rwkv_prep",
    )(p_rw, p_rw, shift0, mu, w0, w2, a0, a2, g2)


def _stack(x, first):
    return jnp.concatenate([jnp.where(first, x, 0.0), jnp.where(first, 0.0, x)], axis=0)


def _each(f, *cols):
    return [f(*xs) for xs in zip(*cols)]


def _sum_sel(x, sel, pieces):
    parts = _split2(x) if pieces == 2 else _split3(x)
    n = x.shape[0]
    out = _dg(jnp.concatenate(parts, axis=0), sel, _NN)
    return sum(out[i * n:(i + 1) * n] for i in range(1, len(parts))) + out[:n]


def _unit_lower_inverses(lows, c):
    n = lows[0].shape[0]
    r = lax.broadcasted_iota(jnp.int32, (n, n), 0)
    col = lax.broadcasted_iota(jnp.int32, (n, n), 1)
    eye = jnp.where(r == col, 1.0, 0.0)
    invs = [eye for _ in lows]
    for lvl in range(_log2(c)):
        same = (r >> (lvl + 1)) == (col >> (lvl + 1))
        lower_left = same & (((r >> lvl) & 1) == 1) & (((col >> lvl) & 1) == 0)
        offs = [jnp.where(lower_left, low, 0.0) for low in lows]
        if lvl == 0:
            invs = [eye + off for off in offs]
            continue
        left = _each(lambda inv, off: _dg(inv, off, _NN), invs, offs)
        invs = _each(lambda inv, x: inv + _dg(x, inv, _NN), invs, left)
    return invs


def _rwkv_scan_kernel(r_ref, k_ref, v_ref, lw_ref, al_ref, g_ref, kk_ref, ka_ref, rk_ref,
                      lnw_ref, lnb_ref, s0_ref, o_ref, sT_ref, s_ref, *, c, t_valid, group):
    ci = pl.program_id(2)

    @pl.when(ci == 0)
    def _():
        s_ref[...] = s0_ref[0]

    lanes = [slice(p * LANES, (p + 1) * LANES) for p in range(group)]
    take = lambda ref: [ref[0, :, sl] for sl in lanes]
    par = lambda ref: [ref[:, sl] for sl in lanes]
    n = 2 * c
    first = lax.broadcasted_iota(jnp.int32, (1, LANES), 1) < HEAD_DIM
    valid = (ci * c + lax.broadcasted_iota(jnp.int32, (c, 1), 0)) < t_valid
    pair_sum = _head_pair_sum_matrix()
    pair_mean = _head_pair_sum_matrix(1.0 / HEAD_DIM)
    t_i = lax.broadcasted_iota(jnp.int32, (c, c), 0)
    s_i = lax.broadcasted_iota(jnp.int32, (c, c), 1)
    upto = jnp.where(s_i <= t_i, 1.0, 0.0).astype(BF16)
    tt = lax.broadcasted_iota(jnp.int32, (n, n), 0) & (c - 1)
    ss = lax.broadcasted_iota(jnp.int32, (n, n), 1) & (c - 1)
    strict, incl = ss < tt, ss <= tt
    diag = (lax.broadcasted_iota(jnp.int32, (LANES, LANES), 0)
            == lax.broadcasted_iota(jnp.int32, (LANES, LANES), 1))

    r = take(r_ref)
    k = [jnp.where(valid, x, 0.0) for x in take(k_ref)]
    v = [jnp.where(valid, x, 0.0) for x in take(v_ref)]
    lw = [jnp.where(valid, x, 0.0) for x in take(lw_ref)]
    al = take(al_ref)

    kk = _each(lambda x, w: x * w, k, par(kk_ref))
    norm = [jnp.sqrt(_sum_sel(x * x, pair_sum, 2)) for x in kk]
    kk = _each(lambda x, nr: x / jnp.maximum(nr, KK_EPS), kk, norm)
    k_mod = _each(lambda x, a_, w: x * (1.0 + (a_ - 1.0) * w), k, al, par(ka_ref))
    b = _each(lambda x, a_: x * a_, kk, al)

    def cumsum(x):
        out = _dg(upto, jnp.concatenate(_split3(x), axis=1), _NN)
        return out[:, :LANES] + out[:, LANES:2 * LANES] + out[:, 2 * LANES:]

    cum = [cumsum(x) for x in lw]
    cum_end = [x[c - 1:c, :] for x in cum]
    dec_out = [jnp.exp(-x) for x in cum]
    dec_rest = _each(lambda e, x: jnp.exp(e - x), cum_end, cum)
    a_t = _each(lambda x, cu, l: _stack(-x * jnp.exp(cu - l), first), kk, cum, lw)
    r_t = _each(lambda x, cu: _stack(x * jnp.exp(cu), first), r, cum)
    b_t = _each(lambda x, d: _stack(x * d, first), b, dec_out)
    k_t = _each(lambda x, d: _stack(x * d, first), k_mod, dec_out)
    b_h = _each(lambda x, d: _stack(x * d, first), b, dec_rest)
    k_h = _each(lambda x, d: _stack(x * d, first), k_mod, dec_rest)
    v_s = [_stack(x, first) for x in v]

    prod = _each(lambda a_, r_, b_, k_: _dg(jnp.concatenate([a_, r_], axis=0),
                                            jnp.concatenate([b_, k_], axis=0), _NT), a_t, r_t, b_t, k_t)
    l_ab = [jnp.where(strict, x[:n, :n], 0.0) for x in prod]
    l_ak = [jnp.where(strict, x[:n, n:], 0.0) for x in prod]
    m_r = [jnp.where(jnp.concatenate([incl, incl], axis=1), x[n:], 0.0) for x in prod]

    solve = _unit_lower_inverses(l_ab, c)
    lv = _each(lambda l_, v_: _dg(l_, v_, _NN), l_ak, v_s)
    au = _each(lambda t_, a_, x: _dg(t_, jnp.concatenate([a_, x], axis=1), _NN), solve, a_t, lv)
    rhs = _each(lambda x, v_: jnp.concatenate(
        [x, jnp.concatenate([jnp.zeros_like(v_), v_], axis=1)], axis=0), au, v_s)
    ry = _each(lambda m, x: _dg(m, x, _NN), m_r, rhs)
    gh = _each(lambda b_, k_, x: _dg(jnp.concatenate([b_, k_], axis=0), x, _TN), b_h, k_h, rhs)

    lhs = _each(lambda r_, x, y, e: jnp.concatenate(
        [r_ + x[:, :LANES], jnp.where(diag, jnp.exp(e), 0.0) + y[:, :LANES]], axis=0), r_t, ry, gh, cum_end)
    step = [_dg3(x, s_ref[p]) for p, x in enumerate(lhs)]
    for p in range(group):
        s_ref[p] = step[p][n:] + gh[p][:, LANES:]
    y = _each(lambda st, x: st[:c] + st[c:n] + x[:c, LANES:] + x[c:, LANES:], step, ry)

    mean = [_sum_sel(x, pair_mean, 2) for x in y]
    d = _each(lambda x, m: x - m, y, mean)
    var = [_sum_sel(x * x, pair_mean, 2) for x in d]
    bonus = _each(lambda r_, k_, w: _sum_sel(r_ * k_ * w, pair_sum, 2), r, k_mod, par(rk_ref))
    for p, sl in enumerate(lanes):
        yn = d[p] * lax.rsqrt(var[p] + GN_EPS) * lnw_ref[:, sl] + lnb_ref[:, sl]
        o_ref[0, :, sl] = ((yn + bonus[p] * v[p]) * g_ref[0, :, sl]).astype(o_ref.dtype)

    @pl.when(ci == pl.num_programs(2) - 1)
    def _():
        sT_ref[0] = s_ref[...]


def _rwkv_scan(r, k, v, lw, al, g, k_k, k_a, r_k, ln_w, ln_b, s0, t_valid):
    b, t_in, d_rw = r.shape
    pairs = d_rw // LANES
    group = SCAN_GROUP if pairs % SCAN_GROUP == 0 else 1
    c = SCAN_CHUNK
    t = -(-t_in // c) * c
    if t != t_in:
        r, k, v, lw, al, g = (jnp.pad(x, ((0, 0), (0, t - t_in), (0, 0))) for x in (r, k, v, lw, al, g))
    width = group * LANES
    row_spec = pl.BlockSpec((1, c, width), lambda bi, p, ci: (bi, ci, p))
    par_spec = pl.BlockSpec((1, width), lambda bi, p, ci: (0, p))
    st_spec = pl.BlockSpec((1, group, LANES, LANES), lambda bi, p, ci: (bi, p, 0, 0))
    o, s_t = pl.pallas_call(
        functools.partial(_rwkv_scan_kernel, c=c, t_valid=t_valid, group=group),
        grid=(b, pairs // group, t // c),
        in_specs=[row_spec] * 6 + [par_spec] * 5 + [st_spec],
        out_specs=[row_spec, st_spec],
        out_shape=[jax.ShapeDtypeStruct((b, t, d_rw), BF16),
                   jax.ShapeDtypeStruct((b, pairs, LANES, LANES), F32)],
        scratch_shapes=[pltpu.VMEM((group, LANES, LANES), F32)],
        compiler_params=_params("parallel", "parallel", "arbitrary"),
        name="rwkv_scan",
    )(r, k, v, lw, al, g, k_k, k_a, r_k, ln_w, ln_b, s0)
    return o[:, :t_in], s_t


def _state_to_pairs(s):
    b, h, e, _ = s.shape
    st = jnp.swapaxes(s, -1, -2).reshape(b, h // 2, 2, e, e)
    z = jnp.zeros_like(st[:, :, 0])
    top = jnp.concatenate([st[:, :, 0], z], axis=-1)
    bot = jnp.concatenate([z, st[:, :, 1]], axis=-1)
    return jnp.concatenate([top, bot], axis=-2)


def _pairs_to_state(sp):
    b, pairs = sp.shape[:2]
    e = HEAD_DIM
    blocks = jnp.stack([sp[:, :, :e, :e], sp[:, :, e:, e:]], axis=2)
    return jnp.swapaxes(blocks, -1, -2).reshape(b, 2 * pairs, e, e)


def _pad_cols(x, width):
    return jnp.pad(x, [(0, 0)] * (x.ndim - 1) + [(0, width - x.shape[-1])])


def _rw_layout(x, d_rw, ranks, widths):
    parts = [x[..., :3 * d_rw]]
    o = 3 * d_rw
    for rank, width in zip(ranks, widths):
        parts.append(_pad_cols(x[..., o:o + rank], width))
        o += rank
    return jnp.concatenate(parts, axis=-1)


def _rw_unlayout(x, d_rw, ranks, widths):
    parts = [x[..., :3 * d_rw]]
    o = 3 * d_rw
    for rank, width in zip(ranks, widths):
        parts.append(x[..., o:o + rank])
        o += width
    return jnp.concatenate(parts, axis=-1)


def _layer(h, lw, cache, s0_pairs, shift0, t_valid, tri):
    b, t, d = h.shape
    d_sb = lw["sb_gain"].shape[-1]
    x = h.reshape(b * t, d)
    x = _ffn(x, lw["ffn1_pre"], lw["ffn1_post"], lw["ffn1_gate"], lw["ffn1_up"], lw["ffn1_down"])
    p_sb = _norm_matmul(x, lw["mix_pre"], lw["w_in_sb"], 1024).reshape(b, t, -1)
    p_rw = _norm_matmul(x, lw["mix_pre"], lw["w_in_rw"], 1280).reshape(b, t, -1)
    if cache is None:
        o_sb = _sb_prompt(p_sb, lw["sb_gain_pairs"], tri)
    else:
        o_sb = _sb_sample(p_sb, cache[0], cache[1], cache[2], lw["sb_gain"], tri)
    r, k, v, lwd, al, g = _rwkv_prep(p_rw, shift0, lw["mu"], lw["w0"], lw["w2"], lw["a0"], lw["a2"],
                                     lw["g2"])
    o_rw, s_t = _rwkv_scan(r, k, v, lwd, al, g, lw["k_k"], lw["k_a"], lw["r_k"], lw["ln_w"],
                           lw["ln_b"], s0_pairs, t_valid)
    x = _mix_out(o_sb.reshape(b * t, -1), o_rw.reshape(b * t, -1), lw["w_out_sb"], lw["w_out_rw"],
                 x, lw["mix_post"])
    x = _ffn(x, lw["ffn2_pre"], lw["ffn2_post"], lw["ffn2_gate"], lw["ffn2_up"], lw["ffn2_down"])
    k_new = p_sb[:, :t_valid, d_sb:2 * d_sb]
    v_new = p_sb[:, :t_valid, 2 * d_sb:]
    return x.reshape(b, t, d), k_new, v_new, s_t, p_rw[:, t_valid - 1]


def kernel(x_prompt, x_sample, cache_sb_k, cache_sb_v, state_rwkv_S, state_rwkv_shift, meta_tokens, ffn1_norm_pre, ffn1_norm_post, ffn1_w_gate, ffn1_w_up, ffn1_w_down, mix_norm_pre, mix_norm_post, w_in, sb_out_gain, rwkv_mu, rwkv_w0, rwkv_w2, rwkv_a0, rwkv_a2, rwkv_g2, rwkv_k_k, rwkv_k_a, rwkv_r_k, rwkv_ln_w, rwkv_ln_b, w_out, ffn2_norm_pre, ffn2_norm_post, ffn2_w_gate, ffn2_w_up, ffn2_w_down):
    depth, d = ffn1_norm_pre.shape
    bp, seq, _ = x_prompt.shape
    bs, dec_seq, _ = x_sample.shape
    n_meta = meta_tokens.shape[0]
    h_sb = sb_out_gain.shape[1]
    d_sb = h_sb * HEAD_DIM
    d_rw = rwkv_w0.shape[1]
    h_rw = d_rw // HEAD_DIM
    ranks = (rwkv_w2.shape[1], rwkv_a2.shape[1], rwkv_g2.shape[1])
    widths = tuple(-(-r // LANES) * LANES for r in ranks)
    p_rw_cols = 3 * d_rw + sum(ranks)
    past = cache_sb_k.shape[2]

    t_valid = n_meta + seq
    t_pad = -(-t_valid // ATTN_BLOCK) * ATTN_BLOCK
    meta = jnp.broadcast_to(meta_tokens[None].astype(x_prompt.dtype), (bp, n_meta, d))
    hp = jnp.concatenate([meta, x_prompt, jnp.zeros((bp, t_pad - t_valid, d), x_prompt.dtype)], axis=1)
    hs = x_sample

    tri = _strict_upper(ATTN_BLOCK)
    feature_major = lambda c: jnp.transpose(c, (0, 1, 3, 4, 2)).reshape(depth, bs, d_sb, past)
    cache_kt, cache_vt = feature_major(cache_sb_k), feature_major(cache_sb_v)
    row = lambda x: x.reshape(1, -1)
    pad_rows = lambda w, width: jnp.pad(w, ((0, width - w.shape[0]), (0, 0)))
    w_in_rw_all = _rw_layout(w_in[:, :, 3 * d_sb:], d_rw, ranks, widths).astype(BF16)
    w_in_sb_all = w_in[:, :, :3 * d_sb].astype(BF16)
    ffn_w = [_to_bf16(w) for w in (ffn1_w_gate, ffn1_w_up, ffn1_w_down, ffn2_w_gate, ffn2_w_up, ffn2_w_down)]
    mu_all = _rw_layout(rwkv_mu, d_rw, ranks, widths)

    outs = [[] for _ in range(8)]
    s0_p = jnp.zeros((bp, h_rw // 2, LANES, LANES), F32)
    shift0_p = jnp.zeros((bp, 1, w_in_rw_all.shape[-1]), F32)
    for l in range(depth):
        lw = dict(
            ffn1_pre=row(ffn1_norm_pre[l]), ffn1_post=row(ffn1_norm_post[l]),
            ffn1_gate=ffn_w[0][l], ffn1_up=ffn_w[1][l], ffn1_down=ffn_w[2][l],
            ffn2_pre=row(ffn2_norm_pre[l]), ffn2_post=row(ffn2_norm_post[l]),
            ffn2_gate=ffn_w[3][l], ffn2_up=ffn_w[4][l], ffn2_down=ffn_w[5][l],
            mix_pre=row(mix_norm_pre[l]), mix_post=row(mix_norm_post[l]),
            w_in_sb=w_in_sb_all[l], w_in_rw=w_in_rw_all[l],
            sb_gain=row(sb_out_gain[l]), sb_gain_pairs=sb_out_gain[l].reshape(h_sb // 2, 1, LANES),
            mu=row(mu_all[l]), w0=row(rwkv_w0[l]), a0=row(rwkv_a0[l]),
            w2=pad_rows(rwkv_w2[l], widths[0]).astype(BF16),
            a2=pad_rows(rwkv_a2[l], widths[1]).astype(BF16),
            g2=pad_rows(rwkv_g2[l], widths[2]).astype(BF16),
            k_k=row(rwkv_k_k[l]), k_a=row(rwkv_k_a[l]), r_k=row(rwkv_r_k[l]),
            ln_w=row(rwkv_ln_w[l]), ln_b=row(rwkv_ln_b[l]),
            w_out_sb=w_out[l, :d_sb].astype(BF16), w_out_rw=w_out[l, d_sb:].astype(BF16),
        )
        hp, kp, vp, sp, shp = _layer(hp, lw, None, s0_p, shift0_p, t_valid, tri)
        cache = (cache_kt, cache_vt, l)
        shift0_s = _rw_layout(state_rwkv_shift[l], d_rw, ranks, widths)[:, None, :]
        hs, ks, vs, ss, shs = _layer(hs, lw, cache, _state_to_pairs(state_rwkv_S[l].astype(F32)),
                                     shift0_s, dec_seq, tri)
        per_layer = (
            kp.reshape(bp, t_valid, h_sb, HEAD_DIM), vp.reshape(bp, t_valid, h_sb, HEAD_DIM),
            _pairs_to_state(sp).astype(state_rwkv_S.dtype),
            _rw_unlayout(shp, d_rw, ranks, widths).astype(state_rwkv_shift.dtype),
            ks.reshape(bs, dec_seq, h_sb, HEAD_DIM), vs.reshape(bs, dec_seq, h_sb, HEAD_DIM),
            _pairs_to_state(ss).astype(state_rwkv_S.dtype),
            _rw_unlayout(shs, d_rw, ranks, widths).astype(state_rwkv_shift.dtype),
        )
        for acc, val in zip(outs, per_layer):
            acc.append(val)
    assert p_rw_cols == state_rwkv_shift.shape[-1]
    y_prompt = hp[:, n_meta:t_valid]
    return (y_prompt, hs) + tuple(jnp.stack(o) for o in outs)
```

```python
import functools

import jax
import jax.numpy as jnp
from jax import lax
from jax.experimental import pallas as pl
from jax.experimental.pallas import tpu as pltpu

F32 = jnp.float32
BF16 = jnp.bfloat16

HEAD_DIM = 64
LANES = 128
NORM_EPS = 1e-6
GN_EPS = 64e-5
KK_EPS = 1e-12
VMEM_LIMIT_BYTES = 56 * 2 ** 20

LOG2E = 1.4426950408889634
ATTN_BLOCK = 256
ATTN_SPLIT = 1
DEAD_LOG = -104.0
ATTN_UNROLL = 4
CACHE_BLOCK = 1024
SCAN_CHUNK = 64
SCAN_GROUP = 8
ROW_TILE = 512
FF_TILE = 512
CAST_BLOCK_BYTES = 8 * 2 ** 20


def _params(*sem):
    return pltpu.CompilerParams(dimension_semantics=sem, vmem_limit_bytes=VMEM_LIMIT_BYTES)


def _tile(n, pref, mult=16):
    t = min(pref, n)
    t -= t % mult
    while t > mult and n % t:
        t -= mult
    assert t >= mult and n % t == 0, (n, pref)
    return t


def _log2(n):
    k = n.bit_length() - 1
    assert 1 << k == n, n
    return k


_NN = (((1,), (0,)), ((), ()))
_NT = (((1,), (1,)), ((), ()))
_TN = (((0,), (0,)), ((), ()))


def _dg(a, b, dims):
    return lax.dot_general(a.astype(BF16), b.astype(BF16), dims, preferred_element_type=F32)


def _split2(x):
    hi = x.astype(BF16)
    lo = (x - hi.astype(F32)).astype(BF16)
    return hi, lo


def _split3(x):
    hi = x.astype(BF16)
    r = x - hi.astype(F32)
    mid = r.astype(BF16)
    lo = (r - mid.astype(F32)).astype(BF16)
    return hi, mid, lo


def _dg3(a, b, dims=_NN):
    ah, al = _split2(a)
    bh, bl = _split2(b)
    return _dg(ah, bh, dims) + (_dg(ah, bl, dims) + _dg(al, bh, dims))


def _dg_sel(x, sel, pieces):
    parts = _split2(x) if pieces == 2 else _split3(x)
    out = _dg(parts[0], sel, _NN)
    for p in parts[1:]:
        out = out + _dg(p, sel, _NN)
    return out


def _sel_dg(sel, x, pieces):
    parts = _split2(x) if pieces == 2 else _split3(x)
    out = _dg(sel, parts[0], _NN)
    for p in parts[1:]:
        out = out + _dg(sel, p, _NN)
    return out


def _rms(x, eps=NORM_EPS):
    return x * lax.rsqrt(jnp.mean(x * x, axis=-1, keepdims=True) + eps)


def _head_pair_sum_matrix(scale=1.0):
    r = lax.broadcasted_iota(jnp.int32, (LANES, LANES), 0) >> _log2(HEAD_DIM)
    c = lax.broadcasted_iota(jnp.int32, (LANES, LANES), 1) >> _log2(HEAD_DIM)
    return jnp.where(r == c, scale, 0.0).astype(BF16)


def _cast_kernel(x_ref, o_ref):
    o_ref[...] = x_ref[...].astype(o_ref.dtype)


def _to_bf16(x):
    flat = x.reshape(-1, x.shape[-1])
    n, w = flat.shape
    tm = _tile(n, max(16, CAST_BLOCK_BYTES // (4 * w)))
    out = pl.pallas_call(
        _cast_kernel,
        grid=(n // tm,),
        in_specs=[pl.BlockSpec((tm, w), lambda i: (i, 0))],
        out_specs=pl.BlockSpec((tm, w), lambda i: (i, 0)),
        out_shape=jax.ShapeDtypeStruct((n, w), BF16),
        compiler_params=_params("parallel"),
        name="to_bf16",
    )(flat)
    return out.reshape(x.shape)


def _ffn_kernel(x_ref, gpre_ref, gpost_ref, wg_ref, wu_ref, wd_ref, o_ref, u_ref, acc_ref):
    f = pl.program_id(1)

    @pl.when(f == 0)
    def _():
        u_ref[...] = (_rms(x_ref[...]) * gpre_ref[...]).astype(BF16)
        acc_ref[...] = jnp.zeros_like(acc_ref)

    u = u_ref[...]
    gate = jnp.dot(u, wg_ref[...], preferred_element_type=F32)
    up = jnp.dot(u, wu_ref[...], preferred_element_type=F32)
    act = (gate * jax.nn.sigmoid(gate) * up).astype(BF16)
    acc_ref[...] += jnp.dot(act, wd_ref[...], preferred_element_type=F32)

    @pl.when(f == pl.num_programs(1) - 1)
    def _():
        o_ref[...] = x_ref[...] + 0.5 * (_rms(acc_ref[...]) * gpost_ref[...])


def _ffn(x, g_pre, g_post, w_gate, w_up, w_down):
    n, d = x.shape
    d_ff = w_gate.shape[1]
    tm = _tile(n, ROW_TILE)
    tf = _tile(d_ff, FF_TILE, LANES)
    return pl.pallas_call(
        _ffn_kernel,
        grid=(n // tm, d_ff // tf),
        in_specs=[
            pl.BlockSpec((tm, d), lambda i, f: (i, 0)),
            pl.BlockSpec((1, d), lambda i, f: (0, 0)),
            pl.BlockSpec((1, d), lambda i, f: (0, 0)),
            pl.BlockSpec((d, tf), lambda i, f: (0, f)),
            pl.BlockSpec((d, tf), lambda i, f: (0, f)),
            pl.BlockSpec((tf, d), lambda i, f: (f, 0)),
        ],
        out_specs=pl.BlockSpec((tm, d), lambda i, f: (i, 0)),
        out_shape=jax.ShapeDtypeStruct((n, d), F32),
        scratch_shapes=[pltpu.VMEM((tm, d), BF16), pltpu.VMEM((tm, d), F32)],
        compiler_params=_params("parallel", "arbitrary"),
        name="ffn_half",
    )(x, g_pre, g_post, w_gate, w_up, w_down)


def _norm_matmul_kernel(x_ref, g_ref, w_ref, o_ref):
    u = (_rms(x_ref[...]) * g_ref[...]).astype(BF16)
    o_ref[...] = jnp.dot(u, w_ref[...], preferred_element_type=F32)


def _norm_matmul(x, g, w, col_tile):
    n, d = x.shape
    cols = w.shape[1]
    tm = _tile(n, ROW_TILE)
    tn = _tile(cols, col_tile, LANES)
    return pl.pallas_call(
        _norm_matmul_kernel,
        grid=(cols // tn, n // tm),
        in_specs=[
            pl.BlockSpec((tm, d), lambda j, i: (i, 0)),
            pl.BlockSpec((1, d), lambda j, i: (0, 0)),
            pl.BlockSpec((d, tn), lambda j, i: (0, j)),
        ],
        out_specs=pl.BlockSpec((tm, tn), lambda j, i: (i, j)),
        out_shape=jax.ShapeDtypeStruct((n, cols), F32),
        compiler_params=_params("parallel", "parallel"),
        name="mix_in",
    )(x, g, w)


def _mix_out_kernel(osb_ref, orw_ref, wsb_ref, wrw_ref, h_ref, g_ref, o_ref):
    o = jnp.dot(osb_ref[...], wsb_ref[...], preferred_element_type=F32)
    o = o + jnp.dot(orw_ref[...], wrw_ref[...], preferred_element_type=F32)
    o_ref[...] = h_ref[...] + _rms(o) * g_ref[...]


def _mix_out(o_sb, o_rw, w_sb, w_rw, h, g):
    n, d = h.shape
    tm = _tile(n, ROW_TILE // 2)
    return pl.pallas_call(
        _mix_out_kernel,
        grid=(n // tm,),
        in_specs=[
            pl.BlockSpec((tm, o_sb.shape[1]), lambda i: (i, 0)),
            pl.BlockSpec((tm, o_rw.shape[1]), lambda i: (i, 0)),
            pl.BlockSpec(w_sb.shape, lambda i: (0, 0)),
            pl.BlockSpec(w_rw.shape, lambda i: (0, 0)),
            pl.BlockSpec((tm, d), lambda i: (i, 0)),
            pl.BlockSpec((1, d), lambda i: (0, 0)),
        ],
        out_specs=pl.BlockSpec((tm, d), lambda i: (i, 0)),
        out_shape=jax.ShapeDtypeStruct((n, d), F32),
        compiler_params=_params("parallel"),
        name="mix_out",
    )(o_sb, o_rw, w_sb, w_rw, h, g)


def _strict_upper(n):
    r = lax.broadcasted_iota(jnp.int32, (n, n), 0)
    c = lax.broadcasted_iota(jnp.int32, (n, n), 1)
    return jnp.where(r > c, 1.0, 0.0).astype(BF16)


def _split_top16(x):
    bits = lax.bitcast_convert_type(x, jnp.uint32) & jnp.uint32(0xFFFF0000)
    hi = lax.bitcast_convert_type(bits, F32)
    return hi.astype(BF16), (x - hi).astype(BF16)


def _sb_sweep(qs, kv_blocks, tri, runs, masks, feature_major=False):
    jobs = []
    for k_blk, v_blk in kv_blocks:
        for c, (q, mask) in enumerate(zip(qs, masks)):
            z = _dg(q, k_blk, _NN if feature_major else _NT)
            nz = -z
            soft = jnp.log(1.0 + jnp.exp(jnp.minimum(z, nz)))
            stay = jnp.minimum(nz, 0.0) - soft
            log_sig = stay + z
            if mask is not None:
                stay = jnp.where(mask, stay, 0.0)
            stacked = jnp.concatenate(_split_top16(stay), axis=0)
            jobs.append((c, v_blk, log_sig, stacked, stay[:, :1]))
    between = [jnp.dot(job[3], tri, preferred_element_type=F32) for job in jobs]
    runs = list(runs)
    outs = [None] * len(qs)
    for (c, v_blk, log_sig, _, stay0), btw in zip(jobs, between):
        rows = log_sig.shape[0]
        right = btw[:rows] + btw[rows:]
        a = jnp.exp(log_sig + right)
        if masks[c] is not None:
            a = jnp.where(masks[c], a, 0.0)
        o = jnp.exp(runs[c]) * _dg(a, v_blk, _NT if feature_major else _NN)
        outs[c] = o if outs[c] is None else outs[c] + o
        runs[c] = runs[c] + (right[:, :1] + stay0)
    return outs, runs


def _head_rms(o, gain, first):
    sq = o * o
    s0 = jnp.sum(jnp.where(first, sq, 0.0), axis=-1, keepdims=True)
    s1 = jnp.sum(jnp.where(first, 0.0, sq), axis=-1, keepdims=True)
    ms = jnp.where(first, s0, s1) * (1.0 / HEAD_DIM)
    return o * lax.rsqrt(ms + NORM_EPS) * gain


def _sb_prompt_kernel(q_ref, k_ref, v_ref, tri_ref, gain_ref, o_ref, acc_ref, run_ref, *, blk):
    qi = pl.program_id(2)
    q = q_ref[0] * (HEAD_DIM ** -0.5)
    first = lax.broadcasted_iota(jnp.int32, (1, LANES), 1) < HEAD_DIM
    q_heads = (jnp.where(first, q, 0.0).astype(BF16), jnp.where(first, 0.0, q).astype(BF16))
    tri = tri_ref[...]
    causal = (lax.broadcasted_iota(jnp.int32, (blk, blk), 1)
              < lax.broadcasted_iota(jnp.int32, (blk, blk), 0))

    acc_ref[...] = jnp.zeros_like(acc_ref)
    run_ref[...] = jnp.zeros_like(run_ref)

    sub = blk // ATTN_SPLIT
    chains = [(h, slice(r * sub, (r + 1) * sub)) for h in range(2) for r in range(ATTN_SPLIT)]

    def sweep(kb, n_blocks, mask):
        kv = []
        for j in range(n_blocks):
            start = pl.multiple_of((kb - j) * blk, blk)
            kv.append((k_ref[0, pl.ds(start, blk), :].astype(BF16),
                       v_ref[0, pl.ds(start, blk), :].astype(BF16)))
        outs, runs = _sb_sweep([q_heads[h][rs] for h, rs in chains], kv, tri,
                               [run_ref[h, rs] for h, rs in chains],
                               [None if mask is None else mask[rs] for h, rs in chains])
        for (h, rs), o, run in zip(chains, outs, runs):
            run_ref[h, rs] = run
            acc_ref[h, rs] += o

    sweep(qi, 1, causal)

    def alive():
        return jnp.max(run_ref[...]) > DEAD_LOG

    rest = qi % ATTN_UNROLL
    singles = jnp.where(rest > 0, rest, jnp.minimum(qi, ATTN_UNROLL))

    def one(carry):
        i, _ = carry
        sweep(qi - 1 - i, 1, None)
        return i + 1, alive()

    _, live = lax.while_loop(lambda c: (c[0] < singles) & c[1], one, (jnp.int32(0), qi >= 0))
    groups = (qi - singles) // ATTN_UNROLL

    def many(carry):
        g, _ = carry
        sweep(qi - 1 - singles - g * ATTN_UNROLL, ATTN_UNROLL, None)
        return g + 1, alive()

    lax.while_loop(lambda c: (c[0] < groups) & c[1], many, (jnp.int32(0), live))
    o = jnp.where(first, acc_ref[0], acc_ref[1])
    o_ref[0] = _head_rms(o, gain_ref[0], first).astype(o_ref.dtype)


def _sb_prompt(p_sb, gain, tri):
    b, t, w = p_sb.shape
    d_sb = w // 3
    pairs = d_sb // LANES
    blk = ATTN_BLOCK
    assert t % blk == 0
    return pl.pallas_call(
        functools.partial(_sb_prompt_kernel, blk=blk),
        grid=(b, pairs, t // blk),
        in_specs=[
            pl.BlockSpec((1, blk, LANES), lambda bi, p, qi: (bi, qi, p)),
            pl.BlockSpec((1, t, LANES), lambda bi, p, qi: (bi, 0, pairs + p)),
            pl.BlockSpec((1, t, LANES), lambda bi, p, qi: (bi, 0, 2 * pairs + p)),
            pl.BlockSpec((blk, blk), lambda bi, p, qi: (0, 0)),
            pl.BlockSpec((1, 1, LANES), lambda bi, p, qi: (p, 0, 0)),
        ],
        out_specs=pl.BlockSpec((1, blk, LANES), lambda bi, p, qi: (bi, qi, p)),
        out_shape=jax.ShapeDtypeStruct((b, t, d_sb), BF16),
        scratch_shapes=[pltpu.VMEM((2, blk, LANES), F32), pltpu.VMEM((2, blk, 1), F32)],
        compiler_params=_params("parallel", "parallel", "arbitrary"),
        name="sb_prompt",
    )(p_sb, p_sb, p_sb, tri, gain)


def _sb_sample_kernel(q_ref, kn_ref, vn_ref, kc_ref, vc_ref, tri_ref, gain_ref, o_ref,
                      qs_ref, acc_ref, run_ref, *, n_q, n_heads):
    j = pl.program_id(1)
    rows = n_heads * n_q
    d_sb = n_heads * HEAD_DIM
    row_head = lax.broadcasted_iota(jnp.int32, (rows, 1), 0) >> _log2(n_q)

    @pl.when(j == 0)
    def _():
        lane_head = lax.broadcasted_iota(jnp.int32, (1, d_sb), 1) >> _log2(HEAD_DIM)
        q = jnp.tile(q_ref[0] * (HEAD_DIM ** -0.5), (n_heads, 1))
        qs = jnp.where(row_head == lane_head, q, 0.0).astype(BF16)
        qs_ref[...] = qs
        n_new = kn_ref.shape[1]
        q_pos = lax.broadcasted_iota(jnp.int32, (rows, 1), 0) & (n_q - 1)
        mask = lax.broadcasted_iota(jnp.int32, (1, n_new), 1) < q_pos
        outs, runs = _sb_sweep([qs], [(kn_ref[0].astype(BF16), vn_ref[0].astype(BF16))],
                               tri_ref[:n_new, :n_new], [jnp.zeros((rows, 1), F32)], [mask])
        run_ref[...] = runs[0]
        acc_ref[...] = outs[0]

    @pl.when(jnp.max(run_ref[...]) > DEAD_LOG)
    def _():
        half = rows // 2
        chains = [slice(0, half), slice(half, rows)]
        sub = tri_ref.shape[0]
        starts = range(kc_ref.shape[3] - sub, -1, -sub)
        kv = [(kc_ref[0, 0, :, s:s + sub].astype(BF16), vc_ref[0, 0, :, s:s + sub].astype(BF16))
              for s in starts]
        outs, runs = _sb_sweep([qs_ref[rs, :] for rs in chains], kv, tri_ref[...],
                               [run_ref[rs, :] for rs in chains], [None, None], feature_major=True)
        for rs, o, run in zip(chains, outs, runs):
            run_ref[rs, :] = run
            acc_ref[rs, :] += o

    @pl.when(j == pl.num_programs(1) - 1)
    def _():
        lane_head = lax.broadcasted_iota(jnp.int32, (1, d_sb), 1) >> _log2(HEAD_DIM)
        o = jnp.zeros((n_q, d_sb), F32)
        for h in range(n_heads):
            o = o + jnp.where(lane_head == h, acc_ref[h * n_q:(h + 1) * n_q, :], 0.0)
        first = (lax.broadcasted_iota(jnp.int32, (1, LANES), 1) < HEAD_DIM)
        for p in range(d_sb // LANES):
            sl = slice(p * LANES, (p + 1) * LANES)
            o_ref[0, :, sl] = _head_rms(o[:, sl], gain_ref[:, sl], first).astype(o_ref.dtype)


def _sb_sample(p_sb, cache_k, cache_v, layer, gain, tri):
    b, n_q, w = p_sb.shape
    d_sb = w // 3
    n_heads = d_sb // HEAD_DIM
    past = cache_k.shape[3]
    blk = min(CACHE_BLOCK, past)
    assert past % blk == 0 and blk % tri.shape[0] == 0 and n_q <= LANES
    n_blocks = past // blk
    pad = ((0, 0), (0, LANES - n_q), (0, 0))
    k_new = jnp.pad(p_sb[:, :, d_sb:2 * d_sb], pad)
    v_new = jnp.pad(p_sb[:, :, 2 * d_sb:], pad)
    rows = n_heads * n_q
    return pl.pallas_call(
        functools.partial(_sb_sample_kernel, n_q=n_q, n_heads=n_heads),
        grid=(b, n_blocks),
        in_specs=[
            pl.BlockSpec((1, n_q, d_sb), lambda bi, j: (bi, 0, 0)),
            pl.BlockSpec((1, LANES, d_sb), lambda bi, j: (bi, 0, 0)),
            pl.BlockSpec((1, LANES, d_sb), lambda bi, j: (bi, 0, 0)),
            pl.BlockSpec((1, 1, d_sb, blk), lambda bi, j: (layer, bi, 0, n_blocks - 1 - j)),
            pl.BlockSpec((1, 1, d_sb, blk), lambda bi, j: (layer, bi, 0, n_blocks - 1 - j)),
            pl.BlockSpec(tri.shape, lambda bi, j: (0, 0)),
            pl.BlockSpec((1, d_sb), lambda bi, j: (0, 0)),
        ],
        out_specs=pl.BlockSpec((1, n_q, d_sb), lambda bi, j: (bi, 0, 0)),
        out_shape=jax.ShapeDtypeStruct((b, n_q, d_sb), BF16),
        scratch_shapes=[pltpu.VMEM((rows, d_sb), BF16), pltpu.VMEM((rows, d_sb), F32),
                        pltpu.VMEM((rows, 1), F32)],
        compiler_params=_params("parallel", "arbitrary"),
        name="sb_sample",
    )(p_sb, k_new, v_new, cache_k, cache_v, tri, gain)


def _rwkv_prep_kernel(p_ref, tail_ref, shift_ref, mu_ref, w0_ref, w2_ref, a0_ref, a2_ref, g2_ref,
                      r_ref, k_ref, v_ref, lw_ref, al_ref, g_ref, *, d_rw, wd, wa):
    i = pl.program_id(1)
    p = p_ref[0]
    tm = p.shape[0]
    before = jnp.where(i == 0, shift_ref[0], tail_ref[0, 7:8, :])
    row = lax.broadcasted_iota(jnp.int32, (tm, 1), 0)
    prev = jnp.where(row == 0, before, pltpu.roll(p, 1, 0))
    xs = p + (prev - p) * mu_ref[...]
    o1, o2, o3 = d_rw, 2 * d_rw, 3 * d_rw
    r_ref[0] = xs[:, :o1]
    k_ref[0] = xs[:, o1:o2]
    v_ref[0] = xs[:, o2:o3]
    dw = xs[:, o3:o3 + wd]
    da = xs[:, o3 + wd:o3 + wd + wa]
    dg = xs[:, o3 + wd + wa:]
    w_pre = w0_ref[...] + _dg(jnp.tanh(dw), w2_ref[...], _NN)
    w_log = -(jnp.maximum(-w_pre, 0.0) + jnp.log1p(jnp.exp(-jnp.abs(w_pre)))) - 0.5
    lw_ref[0] = -jnp.exp(w_log)
    al_ref[0] = jax.nn.sigmoid(a0_ref[...] + _dg(da, a2_ref[...], _NN))
    g_ref[0] = _dg(jax.nn.sigmoid(dg), g2_ref[...], _NN)


def _rwkv_prep(p_rw, shift0, mu, w0, w2, a0, a2, g2):
    b, t, pw = p_rw.shape
    d_rw = w0.shape[1]
    wd, wa = w2.shape[0], a2.shape[0]
    tm = _tile(t, ROW_TILE // 2)
    tail = max(tm // 8, 1)
    out = jax.ShapeDtypeStruct((b, t, d_rw), F32)
    row_spec = pl.BlockSpec((1, tm, d_rw), lambda bi, i: (bi, i, 0))
    full = lambda a: pl.BlockSpec(a.shape, lambda bi, i: (0,) * a.ndim)
    return pl.pallas_call(
        functools.partial(_rwkv_prep_kernel, d_rw=d_rw, wd=wd, wa=wa),
        grid=(b, t // tm),
        in_specs=[
            pl.BlockSpec((1, tm, pw), lambda bi, i: (bi, i, 0)),
            pl.BlockSpec((1, 8, pw), lambda bi, i: (bi, jnp.maximum(i * tail - 1, 0), 0)),
            pl.BlockSpec((1, 1, pw), lambda bi, i: (bi, 0, 0)),
            full(mu), full(w0), full(w2), full(a0), full(a2), full(g2),
        ],
        out_specs=[row_spec] * 6,
        out_shape=[out] * 6,
        compiler_params=_params("parallel", "parallel"),
        name="rwkv_prep",
    )(p_rw, p_rw, shift0, mu, w0, w2, a0, a2, g2)


def _stack(x, first):
    return jnp.concatenate([jnp.where(first, x, 0.0), jnp.where(first, 0.0, x)], axis=0)


def _each(f, *cols):
    return [f(*xs) for xs in zip(*cols)]


def _sum_sel(x, sel, pieces):
    parts = _split2(x) if pieces == 2 else _split3(x)
    n = x.shape[0]
    out = _dg(jnp.concatenate(parts, axis=0), sel, _NN)
    return sum(out[i * n:(i + 1) * n] for i in range(1, len(parts))) + out[:n]


def _unit_lower_inverses(lows, c):
    n = lows[0].shape[0]
    r = lax.broadcasted_iota(jnp.int32, (n, n), 0)
    col = lax.broadcasted_iota(jnp.int32, (n, n), 1)
    eye = jnp.where(r == col, 1.0, 0.0)
    invs = [eye for _ in lows]
    for lvl in range(_log2(c)):
        same = (r >> (lvl + 1)) == (col >> (lvl + 1))
        lower_left = same & (((r >> lvl) & 1) == 1) & (((col >> lvl) & 1) == 0)
        offs = [jnp.where(lower_left, low, 0.0) for low in lows]
        if lvl == 0:
            invs = [eye + off for off in offs]
            continue
        left = _each(lambda inv, off: _dg(inv, off, _NN), invs, offs)
        invs = _each(lambda inv, x: inv + _dg(x, inv, _NN), invs, left)
    return invs


def _rwkv_scan_kernel(r_ref, k_ref, v_ref, lw_ref, al_ref, g_ref, kk_ref, ka_ref, rk_ref,
                      lnw_ref, lnb_ref, s0_ref, o_ref, sT_ref, s_ref, *, c, t_valid, group):
    ci = pl.program_id(2)

    @pl.when(ci == 0)
    def _():
        s_ref[...] = s0_ref[0]

    lanes = [slice(p * LANES, (p + 1) * LANES) for p in range(group)]
    take = lambda ref: [ref[0, :, sl] for sl in lanes]
    par = lambda ref: [ref[:, sl] for sl in lanes]
    n = 2 * c
    first = lax.broadcasted_iota(jnp.int32, (1, LANES), 1) < HEAD_DIM
    valid = (ci * c + lax.broadcasted_iota(jnp.int32, (c, 1), 0)) < t_valid
    pair_sum = _head_pair_sum_matrix()
    pair_mean = _head_pair_sum_matrix(1.0 / HEAD_DIM)
    t_i = lax.broadcasted_iota(jnp.int32, (c, c), 0)
    s_i = lax.broadcasted_iota(jnp.int32, (c, c), 1)
    upto = jnp.where(s_i <= t_i, 1.0, 0.0).astype(BF16)
    tt = lax.broadcasted_iota(jnp.int32, (n, n), 0) & (c - 1)
    ss = lax.broadcasted_iota(jnp.int32, (n, n), 1) & (c - 1)
    strict, incl = ss < tt, ss <= tt
    diag = (lax.broadcasted_iota(jnp.int32, (LANES, LANES), 0)
            == lax.broadcasted_iota(jnp.int32, (LANES, LANES), 1))

    r = take(r_ref)
    k = [jnp.where(valid, x, 0.0) for x in take(k_ref)]
    v = [jnp.where(valid, x, 0.0) for x in take(v_ref)]
    lw = [jnp.where(valid, x, 0.0) for x in take(lw_ref)]
    al = take(al_ref)

    kk = _each(lambda x, w: x * w, k, par(kk_ref))
    norm = [jnp.sqrt(_sum_sel(x * x, pair_sum, 2)) for x in kk]
    kk = _each(lambda x, nr: x / jnp.maximum(nr, KK_EPS), kk, norm)
    k_mod = _each(lambda x, a_, w: x * (1.0 + (a_ - 1.0) * w), k, al, par(ka_ref))
    b = _each(lambda x, a_: x * a_, kk, al)

    def cumsum(x):
        out = _dg(upto, jnp.concatenate(_split3(x), axis=1), _NN)
        return out[:, :LANES] + out[:, LANES:2 * LANES] + out[:, 2 * LANES:]

    cum = [cumsum(x) for x in lw]
    cum_end = [x[c - 1:c, :] for x in cum]
    dec_out = [jnp.exp(-x) for x in cum]
    dec_rest = _each(lambda e, x: jnp.exp(e - x), cum_end, cum)
    a_t = _each(lambda x, cu, l: _stack(-x * jnp.exp(cu - l), first), kk, cum, lw)
    r_t = _each(lambda x, cu: _stack(x * jnp.exp(cu), first), r, cum)
    b_t = _each(lambda x, d: _stack(x * d, first), b, dec_out)
    k_t = _each(lambda x, d: _stack(x * d, first), k_mod, dec_out)
    b_h = _each(lambda x, d: _stack(x * d, first), b, dec_rest)
    k_h = _each(lambda x, d: _stack(x * d, first), k_mod, dec_rest)
    v_s = [_stack(x, first) for x in v]

    prod = _each(lambda a_, r_, b_, k_: _dg(jnp.concatenate([a_, r_], axis=0),
                                            jnp.concatenate([b_, k_], axis=0), _NT), a_t, r_t, b_t, k_t)
    l_ab = [jnp.where(strict, x[:n, :n], 0.0) for x in prod]
    l_ak = [jnp.where(strict, x[:n, n:], 0.0) for x in prod]
    m_r = [jnp.where(jnp.concatenate([incl, incl], axis=1), x[n:], 0.0) for x in prod]

    solve = _unit_lower_inverses(l_ab, c)
    lv = _each(lambda l_, v_: _dg(l_, v_, _NN), l_ak, v_s)
    au = _each(lambda t_, a_, x: _dg(t_, jnp.concatenate([a_, x], axis=1), _NN), solve, a_t, lv)
    rhs = _each(lambda x, v_: jnp.concatenate(
        [x, jnp.concatenate([jnp.zeros_like(v_), v_], axis=1)], axis=0), au, v_s)
    ry = _each(lambda m, x: _dg(m, x, _NN), m_r, rhs)
    gh = _each(lambda b_, k_, x: _dg(jnp.concatenate([b_, k_], axis=0), x, _TN), b_h, k_h, rhs)

    lhs = _each(lambda r_, x, y, e: jnp.concatenate(
        [r_ + x[:, :LANES], jnp.where(diag, jnp.exp(e), 0.0) + y[:, :LANES]], axis=0), r_t, ry, gh, cum_end)
    step = [_dg3(x, s_ref[p]) for p, x in enumerate(lhs)]
    for p in range(group):
        s_ref[p] = step[p][n:] + gh[p][:, LANES:]
    y = _each(lambda st, x: st[:c] + st[c:n] + x[:c, LANES:] + x[c:, LANES:], step, ry)

    mean = [_sum_sel(x, pair_mean, 2) for x in y]
    d = _each(lambda x, m: x - m, y, mean)
    var = [_sum_sel(x * x, pair_mean, 2) for x in d]
    bonus = _each(lambda r_, k_, w: _sum_sel(r_ * k_ * w, pair_sum, 2), r, k_mod, par(rk_ref))
    for p, sl in enumerate(lanes):
        yn = d[p] * lax.rsqrt(var[p] + GN_EPS) * lnw_ref[:, sl] + lnb_ref[:, sl]
        o_ref[0, :, sl] = ((yn + bonus[p] * v[p]) * g_ref[0, :, sl]).astype(o_ref.dtype)

    @pl.when(ci == pl.num_programs(2) - 1)
    def _():
        sT_ref[0] = s_ref[...]


def _rwkv_scan(r, k, v, lw, al, g, k_k, k_a, r_k, ln_w, ln_b, s0, t_valid):
    b, t_in, d_rw = r.shape
    pairs = d_rw // LANES
    group = SCAN_GROUP if pairs % SCAN_GROUP == 0 else 1
    c = SCAN_CHUNK
    t = -(-t_in // c) * c
    if t != t_in:
        r, k, v, lw, al, g = (jnp.pad(x, ((0, 0), (0, t - t_in), (0, 0))) for x in (r, k, v, lw, al, g))
    width = group * LANES
    row_spec = pl.BlockSpec((1, c, width), lambda bi, p, ci: (bi, ci, p))
    par_spec = pl.BlockSpec((1, width), lambda bi, p, ci: (0, p))
    st_spec = pl.BlockSpec((1, group, LANES, LANES), lambda bi, p, ci: (bi, p, 0, 0))
    o, s_t = pl.pallas_call(
        functools.partial(_rwkv_scan_kernel, c=c, t_valid=t_valid, group=group),
        grid=(b, pairs // group, t // c),
        in_specs=[row_spec] * 6 + [par_spec] * 5 + [st_spec],
        out_specs=[row_spec, st_spec],
        out_shape=[jax.ShapeDtypeStruct((b, t, d_rw), BF16),
                   jax.ShapeDtypeStruct((b, pairs, LANES, LANES), F32)],
        scratch_shapes=[pltpu.VMEM((group, LANES, LANES), F32)],
        compiler_params=_params("parallel", "parallel", "arbitrary"),
        name="rwkv_scan",
    )(r, k, v, lw, al, g, k_k, k_a, r_k, ln_w, ln_b, s0)
    return o[:, :t_in], s_t


def _state_to_pairs(s):
    b, h, e, _ = s.shape
    st = jnp.swapaxes(s, -1, -2).reshape(b, h // 2, 2, e, e)
    z = jnp.zeros_like(st[:, :, 0])
    top = jnp.concatenate([st[:, :, 0], z], axis=-1)
    bot = jnp.concatenate([z, st[:, :, 1]], axis=-1)
    return jnp.concatenate([top, bot], axis=-2)


def _pairs_to_state(sp):
    b, pairs = sp.shape[:2]
    e = HEAD_DIM
    blocks = jnp.stack([sp[:, :, :e, :e], sp[:, :, e:, e:]], axis=2)
    return jnp.swapaxes(blocks, -1, -2).reshape(b, 2 * pairs, e, e)


def _pad_cols(x, width):
    return jnp.pad(x, [(0, 0)] * (x.ndim - 1) + [(0, width - x.shape[-1])])


def _rw_layout(x, d_rw, ranks, widths):
    parts = [x[..., :3 * d_rw]]
    o = 3 * d_rw
    for rank, width in zip(ranks, widths):
        parts.append(_pad_cols(x[..., o:o + rank], width))
        o += rank
    return jnp.concatenate(parts, axis=-1)


def _rw_unlayout(x, d_rw, ranks, widths):
    parts = [x[..., :3 * d_rw]]
    o = 3 * d_rw
    for rank, width in zip(ranks, widths):
        parts.append(x[..., o:o + rank])
        o += width
    return jnp.concatenate(parts, axis=-1)


def _layer(h, lw, cache, s0_pairs, shift0, t_valid, tri):
    b, t, d = h.shape
    d_sb = lw["sb_gain"].shape[-1]
    x = h.reshape(b * t, d)
    x = _ffn(x, lw["ffn1_pre"], lw["ffn1_post"], lw["ffn1_gate"], lw["ffn1_up"], lw["ffn1_down"])
    p_sb = _norm_matmul(x, lw["mix_pre"], lw["w_in_sb"], 1024).reshape(b, t, -1)
    p_rw = _norm_matmul(x, lw["mix_pre"], lw["w_in_rw"], 1280).reshape(b, t, -1)
    if cache is None:
        o_sb = _sb_prompt(p_sb, lw["sb_gain_pairs"], tri)
    else:
        o_sb = _sb_sample(p_sb, cache[0], cache[1], cache[2], lw["sb_gain"], tri)
    r, k, v, lwd, al, g = _rwkv_prep(p_rw, shift0, lw["mu"], lw["w0"], lw["w2"], lw["a0"], lw["a2"],
                                     lw["g2"])
    o_rw, s_t = _rwkv_scan(r, k, v, lwd, al, g, lw["k_k"], lw["k_a"], lw["r_k"], lw["ln_w"],
                           lw["ln_b"], s0_pairs, t_valid)
    x = _mix_out(o_sb.reshape(b * t, -1), o_rw.reshape(b * t, -1), lw["w_out_sb"], lw["w_out_rw"],
                 x, lw["mix_post"])
    x = _ffn(x, lw["ffn2_pre"], lw["ffn2_post"], lw["ffn2_gate"], lw["ffn2_up"], lw["ffn2_down"])
    k_new = p_sb[:, :t_valid, d_sb:2 * d_sb]
    v_new = p_sb[:, :t_valid, 2 * d_sb:]
    return x.reshape(b, t, d), k_new, v_new, s_t, p_rw[:, t_valid - 1]


def kernel(x_prompt, x_sample, cache_sb_k, cache_sb_v, state_rwkv_S, state_rwkv_shift, meta_tokens, ffn1_norm_pre, ffn1_norm_post, ffn1_w_gate, ffn1_w_up, ffn1_w_down, mix_norm_pre, mix_norm_post, w_in, sb_out_gain, rwkv_mu, rwkv_w0, rwkv_w2, rwkv_a0, rwkv_a2, rwkv_g2, rwkv_k_k, rwkv_k_a, rwkv_r_k, rwkv_ln_w, rwkv_ln_b, w_out, ffn2_norm_pre, ffn2_norm_post, ffn2_w_gate, ffn2_w_up, ffn2_w_down):
    depth, d = ffn1_norm_pre.shape
    bp, seq, _ = x_prompt.shape
    bs, dec_seq, _ = x_sample.shape
    n_meta = meta_tokens.shape[0]
    h_sb = sb_out_gain.shape[1]
    d_sb = h_sb * HEAD_DIM
    d_rw = rwkv_w0.shape[1]
    h_rw = d_rw // HEAD_DIM
    ranks = (rwkv_w2.shape[1], rwkv_a2.shape[1], rwkv_g2.shape[1])
    widths = tuple(-(-r // LANES) * LANES for r in ranks)
    p_rw_cols = 3 * d_rw + sum(ranks)
    past = cache_sb_k.shape[2]

    t_valid = n_meta + seq
    t_pad = -(-t_valid // ATTN_BLOCK) * ATTN_BLOCK
    meta = jnp.broadcast_to(meta_tokens[None].astype(x_prompt.dtype), (bp, n_meta, d))
    hp = jnp.concatenate([meta, x_prompt, jnp.zeros((bp, t_pad - t_valid, d), x_prompt.dtype)], axis=1)
    hs = x_sample

    tri = _strict_upper(ATTN_BLOCK)
    feature_major = lambda c: jnp.transpose(c, (0, 1, 3, 4, 2)).reshape(depth, bs, d_sb, past)
    cache_kt, cache_vt = feature_major(cache_sb_k), feature_major(cache_sb_v)
    row = lambda x: x.reshape(1, -1)
    pad_rows = lambda w, width: jnp.pad(w, ((0, width - w.shape[0]), (0, 0)))
    w_in_rw_all = _rw_layout(w_in[:, :, 3 * d_sb:], d_rw, ranks, widths).astype(BF16)
    w_in_sb_all = w_in[:, :, :3 * d_sb].astype(BF16)
    ffn_w = [_to_bf16(w) for w in (ffn1_w_gate, ffn1_w_up, ffn1_w_down, ffn2_w_gate, ffn2_w_up, ffn2_w_down)]
    mu_all = _rw_layout(rwkv_mu, d_rw, ranks, widths)

    outs = [[] for _ in range(8)]
    s0_p = jnp.zeros((bp, h_rw // 2, LANES, LANES), F32)
    shift0_p = jnp.zeros((bp, 1, w_in_rw_all.shape[-1]), F32)
    for l in range(depth):
        lw = dict(
            ffn1_pre=row(ffn1_norm_pre[l]), ffn1_post=row(ffn1_norm_post[l]),
            ffn1_gate=ffn_w[0][l], ffn1_up=ffn_w[1][l], ffn1_down=ffn_w[2][l],
            ffn2_pre=row(ffn2_norm_pre[l]), ffn2_post=row(ffn2_norm_post[l]),
            ffn2_gate=ffn_w[3][l], ffn2_up=ffn_w[4][l], ffn2_down=ffn_w[5][l],
            mix_pre=row(mix_norm_pre[l]), mix_post=row(mix_norm_post[l]),
            w_in_sb=w_in_sb_all[l], w_in_rw=w_in_rw_all[l],
            sb_gain=row(sb_out_gain[l]), sb_gain_pairs=sb_out_gain[l].reshape(h_sb // 2, 1, LANES),
            mu=row(mu_all[l]), w0=row(rwkv_w0[l]), a0=row(rwkv_a0[l]),
            w2=pad_rows(rwkv_w2[l], widths[0]).astype(BF16),
            a2=pad_rows(rwkv_a2[l], widths[1]).astype(BF16),
            g2=pad_rows(rwkv_g2[l], widths[2]).astype(BF16),
            k_k=row(rwkv_k_k[l]), k_a=row(rwkv_k_a[l]), r_k=row(rwkv_r_k[l]),
            ln_w=row(rwkv_ln_w[l]), ln_b=row(rwkv_ln_b[l]),
            w_out_sb=w_out[l, :d_sb].astype(BF16), w_out_rw=w_out[l, d_sb:].astype(BF16),
        )
        hp, kp, vp, sp, shp = _layer(hp, lw, None, s0_p, shift0_p, t_valid, tri)
        cache = (cache_kt, cache_vt, l)
        shift0_s = _rw_layout(state_rwkv_shift[l], d_rw, ranks, widths)[:, None, :]
        hs, ks, vs, ss, shs = _layer(hs, lw, cache, _state_to_pairs(state_rwkv_S[l].astype(F32)),
                                     shift0_s, dec_seq, tri)
        per_layer = (
            kp.reshape(bp, t_valid, h_sb, HEAD_DIM), vp.reshape(bp, t_valid, h_sb, HEAD_DIM),
            _pairs_to_state(sp).astype(state_rwkv_S.dtype),
            _rw_unlayout(shp, d_rw, ranks, widths).astype(state_rwkv_shift.dtype),
            ks.reshape(bs, dec_seq, h_sb, HEAD_DIM), vs.reshape(bs, dec_seq, h_sb, HEAD_DIM),
            _pairs_to_state(ss).astype(state_rwkv_S.dtype),
            _rw_unlayout(shs, d_rw, ranks, widths).astype(state_rwkv_shift.dtype),
        )
        for acc, val in zip(outs, per_layer):
            acc.append(val)
    assert p_rw_cols == state_rwkv_shift.shape[-1]
    y_prompt = hp[:, n_meta:t_valid]
    return (y_prompt, hs) + tuple(jnp.stack(o) for o in outs)
```

```python
import functools

import jax
import jax.numpy as jnp
from jax import lax
from jax.experimental import pallas as pl
from jax.experimental.pallas import tpu as pltpu

F32 = jnp.float32
BF16 = jnp.bfloat16

HEAD_DIM = 64
LANES = 128
NORM_EPS = 1e-6
GN_EPS = 64e-5
KK_EPS = 1e-12
VMEM_LIMIT_BYTES = 56 * 2 ** 20

LOG2E = 1.4426950408889634
ATTN_BLOCK = 256
ATTN_GROUP = 2
MIX_IN_COLS = 1664
DEAD_LOG = -104.0
ATTN_UNROLL = 2
CACHE_BLOCK = 512
SCAN_CHUNK = 64
SCAN_GROUP = 8
ROW_TILE = 512
FF_TILE = 512
CAST_BLOCK_BYTES = 8 * 2 ** 20


def _params(*sem):
    return pltpu.CompilerParams(dimension_semantics=sem, vmem_limit_bytes=VMEM_LIMIT_BYTES)


def _tile(n, pref, mult=16):
    t = min(pref, n)
    t -= t % mult
    while t > mult and n % t:
        t -= mult
    assert t >= mult and n % t == 0, (n, pref)
    return t


def _log2(n):
    k = n.bit_length() - 1
    assert 1 << k == n, n
    return k


_NN = (((1,), (0,)), ((), ()))
_NT = (((1,), (1,)), ((), ()))
_TN = (((0,), (0,)), ((), ()))


def _dg(a, b, dims):
    return lax.dot_general(a.astype(BF16), b.astype(BF16), dims, preferred_element_type=F32)


def _split2(x):
    hi = x.astype(BF16)
    lo = (x - hi.astype(F32)).astype(BF16)
    return hi, lo


def _split3(x):
    hi = x.astype(BF16)
    r = x - hi.astype(F32)
    mid = r.astype(BF16)
    lo = (r - mid.astype(F32)).astype(BF16)
    return hi, mid, lo


def _dg3(a, b, dims=_NN):
    ah, al = _split2(a)
    bh, bl = _split2(b)
    return _dg(ah, bh, dims) + (_dg(ah, bl, dims) + _dg(al, bh, dims))


def _dg_sel(x, sel, pieces):
    parts = _split2(x) if pieces == 2 else _split3(x)
    out = _dg(parts[0], sel, _NN)
    for p in parts[1:]:
        out = out + _dg(p, sel, _NN)
    return out


def _sel_dg(sel, x, pieces):
    parts = _split2(x) if pieces == 2 else _split3(x)
    out = _dg(sel, parts[0], _NN)
    for p in parts[1:]:
        out = out + _dg(sel, p, _NN)
    return out


def _rms(x, eps=NORM_EPS):
    return x * lax.rsqrt(jnp.mean(x * x, axis=-1, keepdims=True) + eps)


def _head_pair_sum_matrix(scale=1.0):
    r = lax.broadcasted_iota(jnp.int32, (LANES, LANES), 0) >> _log2(HEAD_DIM)
    c = lax.broadcasted_iota(jnp.int32, (LANES, LANES), 1) >> _log2(HEAD_DIM)
    return jnp.where(r == c, scale, 0.0).astype(BF16)


def _cast_kernel(x_ref, o_ref):
    o_ref[...] = x_ref[...].astype(o_ref.dtype)


def _to_bf16(x):
    flat = x.reshape(-1, x.shape[-1])
    n, w = flat.shape
    tm = _tile(n, max(16, CAST_BLOCK_BYTES // (4 * w)))
    out = pl.pallas_call(
        _cast_kernel,
        grid=(n // tm,),
        in_specs=[pl.BlockSpec((tm, w), lambda i: (i, 0))],
        out_specs=pl.BlockSpec((tm, w), lambda i: (i, 0)),
        out_shape=jax.ShapeDtypeStruct((n, w), BF16),
        compiler_params=_params("parallel"),
        name="to_bf16",
    )(flat)
    return out.reshape(x.shape)


def _ffn_kernel(x_ref, gpre_ref, gpost_ref, wg_ref, wu_ref, wd_ref, o_ref, u_ref, acc_ref):
    f = pl.program_id(1)

    @pl.when(f == 0)
    def _():
        u_ref[...] = (_rms(x_ref[...]) * gpre_ref[...]).astype(BF16)
        acc_ref[...] = jnp.zeros_like(acc_ref)

    u = u_ref[...]
    gate = jnp.dot(u, wg_ref[...], preferred_element_type=F32)
    up = jnp.dot(u, wu_ref[...], preferred_element_type=F32)
    act = (gate * jax.nn.sigmoid(gate) * up).astype(BF16)
    acc_ref[...] += jnp.dot(act, wd_ref[...], preferred_element_type=F32)

    @pl.when(f == pl.num_programs(1) - 1)
    def _():
        o_ref[...] = x_ref[...] + 0.5 * (_rms(acc_ref[...]) * gpost_ref[...])


def _ffn(x, g_pre, g_post, w_gate, w_up, w_down):
    n, d = x.shape
    d_ff = w_gate.shape[1]
    tm = _tile(n, ROW_TILE)
    tf = _tile(d_ff, FF_TILE, LANES)
    return pl.pallas_call(
        _ffn_kernel,
        grid=(n // tm, d_ff // tf),
        in_specs=[
            pl.BlockSpec((tm, d), lambda i, f: (i, 0)),
            pl.BlockSpec((1, d), lambda i, f: (0, 0)),
            pl.BlockSpec((1, d), lambda i, f: (0, 0)),
            pl.BlockSpec((d, tf), lambda i, f: (0, f)),
            pl.BlockSpec((d, tf), lambda i, f: (0, f)),
            pl.BlockSpec((tf, d), lambda i, f: (f, 0)),
        ],
        out_specs=pl.BlockSpec((tm, d), lambda i, f: (i, 0)),
        out_shape=jax.ShapeDtypeStruct((n, d), F32),
        scratch_shapes=[pltpu.VMEM((tm, d), BF16), pltpu.VMEM((tm, d), F32)],
        compiler_params=_params("parallel", "arbitrary"),
        name="ffn_half",
    )(x, g_pre, g_post, w_gate, w_up, w_down)


def _norm_matmul_kernel(x_ref, g_ref, w_ref, o_ref):
    u = (_rms(x_ref[...]) * g_ref[...]).astype(BF16)
    o_ref[...] = jnp.dot(u, w_ref[...], preferred_element_type=F32)


def _norm_matmul(x, g, w, col_tile):
    n, d = x.shape
    cols = w.shape[1]
    tm = _tile(n, ROW_TILE)
    tn = _tile(cols, col_tile, LANES)
    return pl.pallas_call(
        _norm_matmul_kernel,
        grid=(cols // tn, n // tm),
        in_specs=[
            pl.BlockSpec((tm, d), lambda j, i: (i, 0)),
            pl.BlockSpec((1, d), lambda j, i: (0, 0)),
            pl.BlockSpec((d, tn), lambda j, i: (0, j)),
        ],
        out_specs=pl.BlockSpec((tm, tn), lambda j, i: (i, j)),
        out_shape=jax.ShapeDtypeStruct((n, cols), F32),
        compiler_params=_params("parallel", "parallel"),
        name="mix_in",
    )(x, g, w)


def _mix_out_kernel(osb_ref, orw_ref, wsb_ref, wrw_ref, h_ref, g_ref, o_ref):
    o = jnp.dot(osb_ref[...], wsb_ref[...], preferred_element_type=F32)
    o = o + jnp.dot(orw_ref[...], wrw_ref[...], preferred_element_type=F32)
    o_ref[...] = h_ref[...] + _rms(o) * g_ref[...]


def _mix_out(o_sb, o_rw, w_sb, w_rw, h, g):
    n, d = h.shape
    tm = _tile(n, ROW_TILE // 2)
    return pl.pallas_call(
        _mix_out_kernel,
        grid=(n // tm,),
        in_specs=[
            pl.BlockSpec((tm, o_sb.shape[1]), lambda i: (i, 0)),
            pl.BlockSpec((tm, o_rw.shape[1]), lambda i: (i, 0)),
            pl.BlockSpec(w_sb.shape, lambda i: (0, 0)),
            pl.BlockSpec(w_rw.shape, lambda i: (0, 0)),
            pl.BlockSpec((tm, d), lambda i: (i, 0)),
            pl.BlockSpec((1, d), lambda i: (0, 0)),
        ],
        out_specs=pl.BlockSpec((tm, d), lambda i: (i, 0)),
        out_shape=jax.ShapeDtypeStruct((n, d), F32),
        compiler_params=_params("parallel"),
        name="mix_out",
    )(o_sb, o_rw, w_sb, w_rw, h, g)


def _strict_upper(n):
    r = lax.broadcasted_iota(jnp.int32, (n, n), 0)
    c = lax.broadcasted_iota(jnp.int32, (n, n), 1)
    return jnp.where(r > c, 1.0, 0.0).astype(BF16)


def _split_top16(x):
    bits = lax.bitcast_convert_type(x, jnp.uint32) & jnp.uint32(0xFFFF0000)
    hi = lax.bitcast_convert_type(bits, F32)
    return hi.astype(BF16), (x - hi).astype(BF16)


def _sb_sweep(qs, kv_blocks, tri, runs, masks, feature_major=False):
    jobs = []
    for block in kv_blocks:
        for c, (q, mask) in enumerate(zip(qs, masks)):
            k_blk, v_blk = block[c] if isinstance(block, list) else block
            z = _dg(q, k_blk, _NN if feature_major else _NT)
            nz = -z
            soft = jnp.log(1.0 + jnp.exp(jnp.minimum(z, nz)))
            stay = jnp.minimum(nz, 0.0) - soft
            log_sig = stay + z
            if mask is not None:
                stay = jnp.where(mask, stay, 0.0)
            stacked = jnp.concatenate(_split_top16(stay), axis=0)
            jobs.append((c, v_blk, log_sig, stacked, stay[:, :1]))
    between = [jnp.dot(job[3], tri, preferred_element_type=F32) for job in jobs]
    runs = list(runs)
    outs = [None] * len(qs)
    for (c, v_blk, log_sig, _, stay0), btw in zip(jobs, between):
        rows = log_sig.shape[0]
        right = btw[:rows] + btw[rows:]
        a = jnp.exp(log_sig + right)
        if masks[c] is not None:
            a = jnp.where(masks[c], a, 0.0)
        o = jnp.exp(runs[c]) * _dg(a, v_blk, _NT if feature_major else _NN)
        outs[c] = o if outs[c] is None else outs[c] + o
        runs[c] = runs[c] + (right[:, :1] + stay0)
    return outs, runs


def _head_rms(o, gain, first):
    sq = o * o
    s0 = jnp.sum(jnp.where(first, sq, 0.0), axis=-1, keepdims=True)
    s1 = jnp.sum(jnp.where(first, 0.0, sq), axis=-1, keepdims=True)
    ms = jnp.where(first, s0, s1) * (1.0 / HEAD_DIM)
    return o * lax.rsqrt(ms + NORM_EPS) * gain


def _sb_prompt_kernel(q_ref, k_ref, v_ref, tri_ref, gain_ref, o_ref, acc_ref, run_ref, *, blk, group):
    qi = pl.program_id(2)
    first = lax.broadcasted_iota(jnp.int32, (1, LANES), 1) < HEAD_DIM
    lanes = [slice(p * LANES, (p + 1) * LANES) for p in range(group)]
    q_heads = []
    for sl in lanes:
        q = q_ref[0, :, sl] * (HEAD_DIM ** -0.5)
        q_heads += [jnp.where(first, q, 0.0).astype(BF16), jnp.where(first, 0.0, q).astype(BF16)]
    n_chains = len(q_heads)
    tri = tri_ref[...]
    causal = (lax.broadcasted_iota(jnp.int32, (blk, blk), 1)
              < lax.broadcasted_iota(jnp.int32, (blk, blk), 0))

    acc_ref[...] = jnp.zeros_like(acc_ref)
    run_ref[...] = jnp.zeros_like(run_ref)

    def sweep(kb, n_blocks, mask):
        kv = []
        for j in range(n_blocks):
            start = pl.multiple_of((kb - j) * blk, blk)
            k_all = k_ref[0, pl.ds(start, blk), :].astype(BF16)
            v_all = v_ref[0, pl.ds(start, blk), :].astype(BF16)
            kv.append([(k_all[:, lanes[c // 2]], v_all[:, lanes[c // 2]]) for c in range(n_chains)])
        outs, runs = _sb_sweep(q_heads, kv, tri, [run_ref[c] for c in range(n_chains)],
                               [mask] * n_chains)
        for c in range(n_chains):
            run_ref[c] = runs[c]
            acc_ref[c] += outs[c]

    sweep(qi, 1, causal)

    def alive():
        return jnp.max(run_ref[...]) > DEAD_LOG

    rest = qi % ATTN_UNROLL
    singles = jnp.where(rest > 0, rest, jnp.minimum(qi, ATTN_UNROLL))

    def one(carry):
        i, _ = carry
        sweep(qi - 1 - i, 1, None)
        return i + 1, alive()

    _, live = lax.while_loop(lambda c: (c[0] < singles) & c[1], one, (jnp.int32(0), qi >= 0))
    groups = (qi - singles) // ATTN_UNROLL

    def many(carry):
        g, _ = carry
        sweep(qi - 1 - singles - g * ATTN_UNROLL, ATTN_UNROLL, None)
        return g + 1, alive()

    lax.while_loop(lambda c: (c[0] < groups) & c[1], many, (jnp.int32(0), live))
    for p, sl in enumerate(lanes):
        o = jnp.where(first, acc_ref[2 * p], acc_ref[2 * p + 1])
        o_ref[0, :, sl] = _head_rms(o, gain_ref[0, :, sl], first).astype(o_ref.dtype)


def _sb_prompt(p, col0, d_sb, gain, tri):
    b, t, _ = p.shape
    group = ATTN_GROUP
    width = group * LANES
    blk = ATTN_BLOCK
    assert t % blk == 0 and d_sb % width == 0 and col0 % width == 0
    n_groups = d_sb // width
    c0 = col0 // width
    return pl.pallas_call(
        functools.partial(_sb_prompt_kernel, blk=blk, group=group),
        grid=(b, n_groups, t // blk),
        in_specs=[
            pl.BlockSpec((1, blk, width), lambda bi, g, qi: (bi, qi, c0 + g)),
            pl.BlockSpec((1, t, width), lambda bi, g, qi: (bi, 0, c0 + n_groups + g)),
            pl.BlockSpec((1, t, width), lambda bi, g, qi: (bi, 0, c0 + 2 * n_groups + g)),
            pl.BlockSpec((blk, blk), lambda bi, g, qi: (0, 0)),
            pl.BlockSpec((1, 1, width), lambda bi, g, qi: (g, 0, 0)),
        ],
        out_specs=pl.BlockSpec((1, blk, width), lambda bi, g, qi: (bi, qi, g)),
        out_shape=jax.ShapeDtypeStruct((b, t, d_sb), BF16),
        scratch_shapes=[pltpu.VMEM((2 * group, blk, LANES), F32), pltpu.VMEM((2 * group, blk, 1), F32)],
        compiler_params=_params("parallel", "parallel", "arbitrary"),
        name="sb_prompt",
    )(p, p, p, tri, gain.reshape(n_groups, 1, width))


def _sb_sample_kernel(q_ref, kn_ref, vn_ref, kc_hbm, vc_hbm, tri_ref, gain_ref, o_ref,
                      kbuf, vbuf, sem, qs_ref, acc_ref, run_ref, *, n_q, n_heads, layer, blk, n_blocks):
    bi = pl.program_id(0)
    rows = n_heads * n_q
    d_sb = n_heads * HEAD_DIM
    past = n_blocks * blk

    def fetch(i, slot):
        start = pl.multiple_of(past - (i + 1) * blk, blk)
        src = lambda hbm: hbm.at[layer, bi, :, pl.ds(start, blk)]
        return (pltpu.make_async_copy(src(kc_hbm), kbuf.at[slot], sem.at[slot, 0]),
                pltpu.make_async_copy(src(vc_hbm), vbuf.at[slot], sem.at[slot, 1]))

    for cp in fetch(0, 0):
        cp.start()

    row_head = lax.broadcasted_iota(jnp.int32, (rows, 1), 0) >> _log2(n_q)
    lane_head = lax.broadcasted_iota(jnp.int32, (1, d_sb), 1) >> _log2(HEAD_DIM)
    q = jnp.tile(q_ref[0] * (HEAD_DIM ** -0.5), (n_heads, 1))
    qs = jnp.where(row_head == lane_head, q, 0.0).astype(BF16)
    qs_ref[...] = qs
    n_new = kn_ref.shape[1]
    q_pos = lax.broadcasted_iota(jnp.int32, (rows, 1), 0) & (n_q - 1)
    mask = lax.broadcasted_iota(jnp.int32, (1, n_new), 1) < q_pos
    outs, runs = _sb_sweep([qs], [(kn_ref[0].astype(BF16), vn_ref[0].astype(BF16))],
                           tri_ref[:n_new, :n_new], [jnp.zeros((rows, 1), F32)], [mask])
    run_ref[...] = runs[0]
    acc_ref[...] = outs[0]

    def alive():
        return jnp.max(run_ref[...]) > DEAD_LOG

    half = rows // 2
    chains = [slice(0, half), slice(half, rows)]
    sub = tri_ref.shape[0]

    def body(carry):
        i, _ = carry
        slot = i & 1
        for cp in fetch(i, slot):
            cp.wait()

        @pl.when(i + 1 < n_blocks)
        def _():
            for cp in fetch(i + 1, 1 - slot):
                cp.start()

        kv = [(kbuf[slot, :, s:s + sub].astype(BF16), vbuf[slot, :, s:s + sub].astype(BF16))
              for s in range(blk - sub, -1, -sub)]
        outs, runs = _sb_sweep([qs_ref[rs, :] for rs in chains], kv, tri_ref[...],
                               [run_ref[rs, :] for rs in chains], [None, None], feature_major=True)
        for rs, o, run in zip(chains, outs, runs):
            run_ref[rs, :] = run
            acc_ref[rs, :] += o
        return i + 1, alive()

    done, _ = lax.while_loop(lambda c: (c[0] < n_blocks) & c[1], body, (jnp.int32(0), alive()))

    @pl.when(done < n_blocks)
    def _():
        for cp in fetch(done, done & 1):
            cp.wait()

    o = jnp.zeros((n_q, d_sb), F32)
    for h in range(n_heads):
        o = o + jnp.where(lane_head == h, acc_ref[h * n_q:(h + 1) * n_q, :], 0.0)
    first = (lax.broadcasted_iota(jnp.int32, (1, LANES), 1) < HEAD_DIM)
    for p in range(d_sb // LANES):
        sl = slice(p * LANES, (p + 1) * LANES)
        o_ref[0, :, sl] = _head_rms(o[:, sl], gain_ref[:, sl], first).astype(o_ref.dtype)


def _sb_sample(p_sb, cache_k, cache_v, layer, gain, tri):
    b, n_q, w = p_sb.shape
    d_sb = w // 3
    n_heads = d_sb // HEAD_DIM
    past = cache_k.shape[3]
    blk = min(CACHE_BLOCK, past)
    assert past % blk == 0 and blk % tri.shape[0] == 0 and n_q <= LANES
    n_blocks = past // blk
    pad = ((0, 0), (0, LANES - n_q), (0, 0))
    k_new = jnp.pad(p_sb[:, :, d_sb:2 * d_sb], pad)
    v_new = jnp.pad(p_sb[:, :, 2 * d_sb:], pad)
    rows = n_heads * n_q
    return pl.pallas_call(
        functools.partial(_sb_sample_kernel, n_q=n_q, n_heads=n_heads, layer=layer, blk=blk,
                          n_blocks=n_blocks),
        grid=(b,),
        in_specs=[
            pl.BlockSpec((1, n_q, d_sb), lambda bi: (bi, 0, 0)),
            pl.BlockSpec((1, LANES, d_sb), lambda bi: (bi, 0, 0)),
            pl.BlockSpec((1, LANES, d_sb), lambda bi: (bi, 0, 0)),
            pl.BlockSpec(memory_space=pl.ANY),
            pl.BlockSpec(memory_space=pl.ANY),
            pl.BlockSpec(tri.shape, lambda bi: (0, 0)),
            pl.BlockSpec((1, d_sb), lambda bi: (0, 0)),
        ],
        out_specs=pl.BlockSpec((1, n_q, d_sb), lambda bi: (bi, 0, 0)),
        out_shape=jax.ShapeDtypeStruct((b, n_q, d_sb), BF16),
        scratch_shapes=[pltpu.VMEM((2, d_sb, blk), cache_k.dtype), pltpu.VMEM((2, d_sb, blk), cache_v.dtype),
                        pltpu.SemaphoreType.DMA((2, 2)),
                        pltpu.VMEM((rows, d_sb), BF16), pltpu.VMEM((rows, d_sb), F32),
                        pltpu.VMEM((rows, 1), F32)],
        compiler_params=_params("arbitrary"),
        name="sb_sample",
    )(p_sb, k_new, v_new, cache_k, cache_v, tri, gain)


def _rwkv_prep_kernel(p_ref, tail_ref, shift_ref, mu_ref, w0_ref, w2_ref, a0_ref, a2_ref, g2_ref,
                      r_ref, k_ref, v_ref, lw_ref, al_ref, g_ref, *, d_rw, wd, wa):
    i = pl.program_id(1)
    p = p_ref[0]
    tm = p.shape[0]
    before = jnp.where(i == 0, shift_ref[0], tail_ref[0, 7:8, :])
    row = lax.broadcasted_iota(jnp.int32, (tm, 1), 0)
    prev = jnp.where(row == 0, before, pltpu.roll(p, 1, 0))
    xs = p + (prev - p) * mu_ref[...]
    o1, o2, o3 = d_rw, 2 * d_rw, 3 * d_rw
    r_ref[0] = xs[:, :o1]
    k_ref[0] = xs[:, o1:o2]
    v_ref[0] = xs[:, o2:o3]
    dw = xs[:, o3:o3 + wd]
    da = xs[:, o3 + wd:o3 + wd + wa]
    dg = xs[:, o3 + wd + wa:]
    w_pre = w0_ref[...] + _dg(jnp.tanh(dw), w2_ref[...], _NN)
    w_log = -(jnp.maximum(-w_pre, 0.0) + jnp.log1p(jnp.exp(-jnp.abs(w_pre)))) - 0.5
    lw_ref[0] = -jnp.exp(w_log)
    al_ref[0] = jax.nn.sigmoid(a0_ref[...] + _dg(da, a2_ref[...], _NN))
    g_ref[0] = _dg(jax.nn.sigmoid(dg), g2_ref[...], _NN)


def _rwkv_prep(p_rw, pw, shift0, mu, w0, w2, a0, a2, g2):
    b, t, _ = p_rw.shape
    d_rw = w0.shape[1]
    wd, wa = w2.shape[0], a2.shape[0]
    tm = _tile(t, ROW_TILE // 2)
    tail = max(tm // 8, 1)
    out = jax.ShapeDtypeStruct((b, t, d_rw), F32)
    row_spec = pl.BlockSpec((1, tm, d_rw), lambda bi, i: (bi, i, 0))
    full = lambda a: pl.BlockSpec(a.shape, lambda bi, i: (0,) * a.ndim)
    return pl.pallas_call(
        functools.partial(_rwkv_prep_kernel, d_rw=d_rw, wd=wd, wa=wa),
        grid=(b, t // tm),
        in_specs=[
            pl.BlockSpec((1, tm, pw), lambda bi, i: (bi, i, 0)),
            pl.BlockSpec((1, 8, pw), lambda bi, i: (bi, jnp.maximum(i * tail - 1, 0), 0)),
            pl.BlockSpec((1, 1, pw), lambda bi, i: (bi, 0, 0)),
            full(mu), full(w0), full(w2), full(a0), full(a2), full(g2),
        ],
        out_specs=[row_spec] * 6,
        out_shape=[out] * 6,
        compiler_params=_params("parallel", "parallel"),
        name="rwkv_prep",
    )(p_rw, p_rw, shift0, mu, w0, w2, a0, a2, g2)


def _stack(x, first):
    return jnp.concatenate([jnp.where(first, x, 0.0), jnp.where(first, 0.0, x)], axis=0)


def _each(f, *cols):
    return [f(*xs) for xs in zip(*cols)]


def _sum_sel(x, sel, pieces):
    parts = _split2(x) if pieces == 2 else _split3(x)
    n = x.shape[0]
    out = _dg(jnp.concatenate(parts, axis=0), sel, _NN)
    return sum(out[i * n:(i + 1) * n] for i in range(1, len(parts))) + out[:n]


def _unit_lower_inverses(lows, c):
    n = lows[0].shape[0]
    r = lax.broadcasted_iota(jnp.int32, (n, n), 0)
    col = lax.broadcasted_iota(jnp.int32, (n, n), 1)
    eye = jnp.where(r == col, 1.0, 0.0)
    invs = [eye for _ in lows]
    for lvl in range(_log2(c)):
        same = (r >> (lvl + 1)) == (col >> (lvl + 1))
        lower_left = same & (((r >> lvl) & 1) == 1) & (((col >> lvl) & 1) == 0)
        offs = [jnp.where(lower_left, low, 0.0) for low in lows]
        if lvl == 0:
            invs = [eye + off for off in offs]
            continue
        left = _each(lambda inv, off: _dg(inv, off, _NN), invs, offs)
        invs = _each(lambda inv, x: inv + _dg(x, inv, _NN), invs, left)
    return invs


def _rwkv_scan_kernel(r_ref, k_ref, v_ref, lw_ref, al_ref, g_ref, kk_ref, ka_ref, rk_ref,
                      lnw_ref, lnb_ref, s0_ref, o_ref, sT_ref, s_ref, *, c, t_valid, group):
    ci = pl.program_id(2)

    @pl.when(ci == 0)
    def _():
        s_ref[...] = s0_ref[0]

    @pl.when(ci * c < t_valid)
    def _():
        _rwkv_scan_chunk(ci, r_ref, k_ref, v_ref, lw_ref, al_ref, g_ref, kk_ref, ka_ref, rk_ref,
                         lnw_ref, lnb_ref, o_ref, s_ref, c=c, t_valid=t_valid, group=group)

    @pl.when(ci * c >= t_valid)
    def _():
        o_ref[...] = jnp.zeros_like(o_ref)

    @pl.when(ci == pl.num_programs(2) - 1)
    def _():
        sT_ref[0] = s_ref[...]


def _rwkv_scan_chunk(ci, r_ref, k_ref, v_ref, lw_ref, al_ref, g_ref, kk_ref, ka_ref, rk_ref,
                     lnw_ref, lnb_ref, o_ref, s_ref, *, c, t_valid, group):
    lanes = [slice(p * LANES, (p + 1) * LANES) for p in range(group)]
    take = lambda ref: [ref[0, :, sl] for sl in lanes]
    par = lambda ref: [ref[:, sl] for sl in lanes]
    n = 2 * c
    first = lax.broadcasted_iota(jnp.int32, (1, LANES), 1) < HEAD_DIM
    valid = (ci * c + lax.broadcasted_iota(jnp.int32, (c, 1), 0)) < t_valid
    pair_sum = _head_pair_sum_matrix()
    pair_mean = _head_pair_sum_matrix(1.0 / HEAD_DIM)
    t_i = lax.broadcasted_iota(jnp.int32, (c, c), 0)
    s_i = lax.broadcasted_iota(jnp.int32, (c, c), 1)
    upto = jnp.where(s_i <= t_i, 1.0, 0.0).astype(BF16)
    tt = lax.broadcasted_iota(jnp.int32, (n, n), 0) & (c - 1)
    ss = lax.broadcasted_iota(jnp.int32, (n, n), 1) & (c - 1)
    strict, incl = ss < tt, ss <= tt
    diag = (lax.broadcasted_iota(jnp.int32, (LANES, LANES), 0)
            == lax.broadcasted_iota(jnp.int32, (LANES, LANES), 1))

    r = take(r_ref)
    k = [jnp.where(valid, x, 0.0) for x in take(k_ref)]
    v = [jnp.where(valid, x, 0.0) for x in take(v_ref)]
    lw = [jnp.where(valid, x, 0.0) for x in take(lw_ref)]
    al = take(al_ref)

    kk = _each(lambda x, w: x * w, k, par(kk_ref))
    norm = [jnp.sqrt(_sum_sel(x * x, pair_sum, 2)) for x in kk]
    kk = _each(lambda x, nr: x / jnp.maximum(nr, KK_EPS), kk, norm)
    k_mod = _each(lambda x, a_, w: x * (1.0 + (a_ - 1.0) * w), k, al, par(ka_ref))
    b = _each(lambda x, a_: x * a_, kk, al)

    def cumsum(x):
        out = _dg(upto, jnp.concatenate(_split3(x), axis=1), _NN)
        return out[:, :LANES] + out[:, LANES:2 * LANES] + out[:, 2 * LANES:]

    cum = [cumsum(x) for x in lw]
    cum_end = [x[c - 1:c, :] for x in cum]
    dec_out = [jnp.exp(-x) for x in cum]
    dec_rest = _each(lambda e, x: jnp.exp(e - x), cum_end, cum)
    a_t = _each(lambda x, cu, l: _stack(-x * jnp.exp(cu - l), first), kk, cum, lw)
    r_t = _each(lambda x, cu: _stack(x * jnp.exp(cu), first), r, cum)
    b_t = _each(lambda x, d: _stack(x * d, first), b, dec_out)
    k_t = _each(lambda x, d: _stack(x * d, first), k_mod, dec_out)
    b_h = _each(lambda x, d: _stack(x * d, first), b, dec_rest)
    k_h = _each(lambda x, d: _stack(x * d, first), k_mod, dec_rest)
    v_s = [_stack(x, first) for x in v]

    prod = _each(lambda a_, r_, b_, k_: _dg(jnp.concatenate([a_, r_], axis=0),
                                            jnp.concatenate([b_, k_], axis=0), _NT), a_t, r_t, b_t, k_t)
    l_ab = [jnp.where(strict, x[:n, :n], 0.0) for x in prod]
    l_ak = [jnp.where(strict, x[:n, n:], 0.0) for x in prod]
    m_r = [jnp.where(jnp.concatenate([incl, incl], axis=1), x[n:], 0.0) for x in prod]

    solve = _unit_lower_inverses(l_ab, c)
    lv = _each(lambda l_, v_: _dg(l_, v_, _NN), l_ak, v_s)
    au = _each(lambda t_, a_, x: _dg(t_, jnp.concatenate([a_, x], axis=1), _NN), solve, a_t, lv)
    rhs = _each(lambda x, v_: jnp.concatenate(
        [x, jnp.concatenate([jnp.zeros_like(v_), v_], axis=1)], axis=0), au, v_s)
    ry = _each(lambda m, x: _dg(m, x, _NN), m_r, rhs)
    gh = _each(lambda b_, k_, x: _dg(jnp.concatenate([b_, k_], axis=0), x, _TN), b_h, k_h, rhs)

    lhs = _each(lambda r_, x, y, e: jnp.concatenate(
        [r_ + x[:, :LANES], jnp.where(diag, jnp.exp(e), 0.0) + y[:, :LANES]], axis=0), r_t, ry, gh, cum_end)
    step = [_dg3(x, s_ref[p]) for p, x in enumerate(lhs)]
    for p in range(group):
        s_ref[p] = step[p][n:] + gh[p][:, LANES:]
    y = _each(lambda st, x: st[:c] + st[c:n] + x[:c, LANES:] + x[c:, LANES:], step, ry)

    mean = [_sum_sel(x, pair_mean, 2) for x in y]
    d = _each(lambda x, m: x - m, y, mean)
    var = [_sum_sel(x * x, pair_mean, 2) for x in d]
    bonus = _each(lambda r_, k_, w: _sum_sel(r_ * k_ * w, pair_sum, 2), r, k_mod, par(rk_ref))
    for p, sl in enumerate(lanes):
        yn = d[p] * lax.rsqrt(var[p] + GN_EPS) * lnw_ref[:, sl] + lnb_ref[:, sl]
        o_ref[0, :, sl] = ((yn + bonus[p] * v[p]) * g_ref[0, :, sl]).astype(o_ref.dtype)


def _rwkv_scan(r, k, v, lw, al, g, k_k, k_a, r_k, ln_w, ln_b, s0, t_valid):
    b, t_in, d_rw = r.shape
    pairs = d_rw // LANES
    group = SCAN_GROUP if pairs % SCAN_GROUP == 0 else 1
    c = SCAN_CHUNK
    t = -(-t_in // c) * c
    if t != t_in:
        r, k, v, lw, al, g = (jnp.pad(x, ((0, 0), (0, t - t_in), (0, 0))) for x in (r, k, v, lw, al, g))
    width = group * LANES
    row_spec = pl.BlockSpec((1, c, width), lambda bi, p, ci: (bi, ci, p))
    par_spec = pl.BlockSpec((1, width), lambda bi, p, ci: (0, p))
    st_spec = pl.BlockSpec((1, group, LANES, LANES), lambda bi, p, ci: (bi, p, 0, 0))
    o, s_t = pl.pallas_call(
        functools.partial(_rwkv_scan_kernel, c=c, t_valid=t_valid, group=group),
        grid=(b, pairs // group, t // c),
        in_specs=[row_spec] * 6 + [par_spec] * 5 + [st_spec],
        out_specs=[row_spec, st_spec],
        out_shape=[jax.ShapeDtypeStruct((b, t, d_rw), BF16),
                   jax.ShapeDtypeStruct((b, pairs, LANES, LANES), F32)],
        scratch_shapes=[pltpu.VMEM((group, LANES, LANES), F32)],
        compiler_params=_params("parallel", "parallel", "arbitrary"),
        name="rwkv_scan",
    )(r, k, v, lw, al, g, k_k, k_a, r_k, ln_w, ln_b, s0)
    return o[:, :t_in], s_t


def _state_to_pairs(s):
    b, h, e, _ = s.shape
    st = jnp.swapaxes(s, -1, -2).reshape(b, h // 2, 2, e, e)
    z = jnp.zeros_like(st[:, :, 0])
    top = jnp.concatenate([st[:, :, 0], z], axis=-1)
    bot = jnp.concatenate([z, st[:, :, 1]], axis=-1)
    return jnp.concatenate([top, bot], axis=-2)


def _pairs_to_state(sp):
    b, pairs = sp.shape[:2]
    e = HEAD_DIM
    blocks = jnp.stack([sp[:, :, :e, :e], sp[:, :, e:, e:]], axis=2)
    return jnp.swapaxes(blocks, -1, -2).reshape(b, 2 * pairs, e, e)


def _pad_cols(x, width):
    return jnp.pad(x, [(0, 0)] * (x.ndim - 1) + [(0, width - x.shape[-1])])


def _rw_layout(x, d_rw, ranks, widths):
    parts = [x[..., :3 * d_rw]]
    o = 3 * d_rw
    for rank, width in zip(ranks, widths):
        parts.append(_pad_cols(x[..., o:o + rank], width))
        o += rank
    return jnp.concatenate(parts, axis=-1)


def _rw_unlayout(x, d_rw, ranks, widths):
    parts = [x[..., :3 * d_rw]]
    o = 3 * d_rw
    for rank, width in zip(ranks, widths):
        parts.append(x[..., o:o + rank])
        o += width
    return jnp.concatenate(parts, axis=-1)


def _layer(h, lw, cache, s0_pairs, shift0, t_valid, tri):
    b, t, d = h.shape
    d_sb = lw["sb_gain"].shape[-1]
    x = h.reshape(b * t, d)
    x = _ffn(x, lw["ffn1_pre"], lw["ffn1_post"], lw["ffn1_gate"], lw["ffn1_up"], lw["ffn1_down"])
    p = _norm_matmul(x, lw["mix_pre"], lw["w_in"], MIX_IN_COLS).reshape(b, t, -1)
    rw_cols = p.shape[-1] - 3 * d_sb
    if cache is None:
        o_sb = _sb_prompt(p, rw_cols, d_sb, lw["sb_gain"], tri)
    else:
        o_sb = _sb_sample(p[:, :, rw_cols:], cache[0], cache[1], cache[2], lw["sb_gain"], tri)
    r, k, v, lwd, al, g = _rwkv_prep(p, rw_cols, shift0, lw["mu"], lw["w0"], lw["w2"], lw["a0"],
                                     lw["a2"], lw["g2"])
    o_rw, s_t = _rwkv_scan(r, k, v, lwd, al, g, lw["k_k"], lw["k_a"], lw["r_k"], lw["ln_w"],
                           lw["ln_b"], s0_pairs, t_valid)
    x = _mix_out(o_sb.reshape(b * t, -1), o_rw.reshape(b * t, -1), lw["w_out_sb"], lw["w_out_rw"],
                 x, lw["mix_post"])
    x = _ffn(x, lw["ffn2_pre"], lw["ffn2_post"], lw["ffn2_gate"], lw["ffn2_up"], lw["ffn2_down"])
    k_new = p[:, :t_valid, rw_cols + d_sb:rw_cols + 2 * d_sb]
    v_new = p[:, :t_valid, rw_cols + 2 * d_sb:]
    return x.reshape(b, t, d), k_new, v_new, s_t, p[:, t_valid - 1, :rw_cols]


def kernel(x_prompt, x_sample, cache_sb_k, cache_sb_v, state_rwkv_S, state_rwkv_shift, meta_tokens, ffn1_norm_pre, ffn1_norm_post, ffn1_w_gate, ffn1_w_up, ffn1_w_down, mix_norm_pre, mix_norm_post, w_in, sb_out_gain, rwkv_mu, rwkv_w0, rwkv_w2, rwkv_a0, rwkv_a2, rwkv_g2, rwkv_k_k, rwkv_k_a, rwkv_r_k, rwkv_ln_w, rwkv_ln_b, w_out, ffn2_norm_pre, ffn2_norm_post, ffn2_w_gate, ffn2_w_up, ffn2_w_down):
    depth, d = ffn1_norm_pre.shape
    bp, seq, _ = x_prompt.shape
    bs, dec_seq, _ = x_sample.shape
    n_meta = meta_tokens.shape[0]
    h_sb = sb_out_gain.shape[1]
    d_sb = h_sb * HEAD_DIM
    d_rw = rwkv_w0.shape[1]
    h_rw = d_rw // HEAD_DIM
    ranks = (rwkv_w2.shape[1], rwkv_a2.shape[1], rwkv_g2.shape[1])
    widths = tuple(-(-r // LANES) * LANES for r in ranks)
    p_rw_cols = 3 * d_rw + sum(ranks)
    past = cache_sb_k.shape[2]

    t_valid = n_meta + seq
    t_pad = -(-t_valid // ATTN_BLOCK) * ATTN_BLOCK
    meta = jnp.broadcast_to(meta_tokens[None].astype(x_prompt.dtype), (bp, n_meta, d))
    hp = jnp.concatenate([meta, x_prompt, jnp.zeros((bp, t_pad - t_valid, d), x_prompt.dtype)], axis=1)
    hs = x_sample

    tri = _strict_upper(ATTN_BLOCK)
    feature_major = lambda c: jnp.transpose(c, (0, 1, 3, 4, 2)).reshape(depth, bs, d_sb, past)
    cache_kt, cache_vt = feature_major(cache_sb_k), feature_major(cache_sb_v)
    row = lambda x: x.reshape(1, -1)
    pad_rows = lambda w, width: jnp.pad(w, ((0, width - w.shape[0]), (0, 0)))
    w_in_rw = _rw_layout(w_in[:, :, 3 * d_sb:], d_rw, ranks, widths)
    w_in_all = jnp.concatenate([w_in_rw, w_in[:, :, :3 * d_sb]], axis=-1).astype(BF16)
    ffn_w = [_to_bf16(w) for w in (ffn1_w_gate, ffn1_w_up, ffn1_w_down, ffn2_w_gate, ffn2_w_up, ffn2_w_down)]
    mu_all = _rw_layout(rwkv_mu, d_rw, ranks, widths)

    outs = [[] for _ in range(8)]
    s0_p = jnp.zeros((bp, h_rw // 2, LANES, LANES), F32)
    shift0_p = jnp.zeros((bp, 1, w_in_rw.shape[-1]), F32)
    for l in range(depth):
        lw = dict(
            ffn1_pre=row(ffn1_norm_pre[l]), ffn1_post=row(ffn1_norm_post[l]),
            ffn1_gate=ffn_w[0][l], ffn1_up=ffn_w[1][l], ffn1_down=ffn_w[2][l],
            ffn2_pre=row(ffn2_norm_pre[l]), ffn2_post=row(ffn2_norm_post[l]),
            ffn2_gate=ffn_w[3][l], ffn2_up=ffn_w[4][l], ffn2_down=ffn_w[5][l],
            mix_pre=row(mix_norm_pre[l]), mix_post=row(mix_norm_post[l]),
            w_in=w_in_all[l], sb_gain=row(sb_out_gain[l]),
            mu=row(mu_all[l]), w0=row(rwkv_w0[l]), a0=row(rwkv_a0[l]),
            w2=pad_rows(rwkv_w2[l], widths[0]).astype(BF16),
            a2=pad_rows(rwkv_a2[l], widths[1]).astype(BF16),
            g2=pad_rows(rwkv_g2[l], widths[2]).astype(BF16),
            k_k=row(rwkv_k_k[l]), k_a=row(rwkv_k_a[l]), r_k=row(rwkv_r_k[l]),
            ln_w=row(rwkv_ln_w[l]), ln_b=row(rwkv_ln_b[l]),
            w_out_sb=w_out[l, :d_sb].astype(BF16), w_out_rw=w_out[l, d_sb:].astype(BF16),
        )
        hp, kp, vp, sp, shp = _layer(hp, lw, None, s0_p, shift0_p, t_valid, tri)
        cache = (cache_kt, cache_vt, l)
        shift0_s = _rw_layout(state_rwkv_shift[l], d_rw, ranks, widths)[:, None, :]
        hs, ks, vs, ss, shs = _layer(hs, lw, cache, _state_to_pairs(state_rwkv_S[l].astype(F32)),
                                     shift0_s, dec_seq, tri)
        per_layer = (
            kp.reshape(bp, t_valid, h_sb, HEAD_DIM), vp.reshape(bp, t_valid, h_sb, HEAD_DIM),
            _pairs_to_state(sp).astype(state_rwkv_S.dtype),
            _rw_unlayout(shp, d_rw, ranks, widths).astype(state_rwkv_shift.dtype),
            ks.reshape(bs, dec_seq, h_sb, HEAD_DIM), vs.reshape(bs, dec_seq, h_sb, HEAD_DIM),
            _pairs_to_state(ss).astype(state_rwkv_S.dtype),
            _rw_unlayout(shs, d_rw, ranks, widths).astype(state_rwkv_shift.dtype),
        )
        for acc, val in zip(outs, per_layer):
            acc.append(val)
    assert p_rw_cols == state_rwkv_shift.shape[-1]
    y_prompt = hp[:, n_meta:t_valid]
    return (y_prompt, hs) + tuple(jnp.stack(o) for o in outs)
```

```python
import functools

import jax
import jax.numpy as jnp
from jax import lax
from jax.experimental import pallas as pl
from jax.experimental.pallas import tpu as pltpu

F32 = jnp.float32
BF16 = jnp.bfloat16

HEAD_DIM = 64
LANES = 128
NORM_EPS = 1e-6
GN_EPS = 64e-5
KK_EPS = 1e-12
VMEM_LIMIT_BYTES = 56 * 2 ** 20

LOG2E = 1.4426950408889634
ATTN_BLOCK = 256
ATTN_GROUP = 2
MIX_IN_COLS = 1664
DEAD_LOG = -104.0
ATTN_UNROLL = 2
CACHE_BLOCK = 512
SCAN_CHUNK = 64
ROW_TILE = 512
FF_TILE = 512
CAST_BLOCK_BYTES = 8 * 2 ** 20


def _params(*sem):
    return pltpu.CompilerParams(dimension_semantics=sem, vmem_limit_bytes=VMEM_LIMIT_BYTES)


def _tile(n, pref, mult=16):
    t = min(pref, n)
    t -= t % mult
    while t > mult and n % t:
        t -= mult
    assert t >= mult and n % t == 0, (n, pref)
    return t


def _log2(n):
    k = n.bit_length() - 1
    assert 1 << k == n, n
    return k


_NN = (((1,), (0,)), ((), ()))
_NT = (((1,), (1,)), ((), ()))
_TN = (((0,), (0,)), ((), ()))


def _dg(a, b, dims):
    return lax.dot_general(a.astype(BF16), b.astype(BF16), dims, preferred_element_type=F32)


def _split2(x):
    hi = x.astype(BF16)
    lo = (x - hi.astype(F32)).astype(BF16)
    return hi, lo


def _split3(x):
    hi = x.astype(BF16)
    r = x - hi.astype(F32)
    mid = r.astype(BF16)
    lo = (r - mid.astype(F32)).astype(BF16)
    return hi, mid, lo


def _dg3(a, b, dims=_NN):
    ah, al = _split2(a)
    bh, bl = _split2(b)
    return _dg(ah, bh, dims) + (_dg(ah, bl, dims) + _dg(al, bh, dims))


def _dg_sel(x, sel, pieces):
    parts = _split2(x) if pieces == 2 else _split3(x)
    out = _dg(parts[0], sel, _NN)
    for p in parts[1:]:
        out = out + _dg(p, sel, _NN)
    return out


def _sel_dg(sel, x, pieces):
    parts = _split2(x) if pieces == 2 else _split3(x)
    out = _dg(sel, parts[0], _NN)
    for p in parts[1:]:
        out = out + _dg(sel, p, _NN)
    return out


def _rms(x, eps=NORM_EPS):
    return x * lax.rsqrt(jnp.mean(x * x, axis=-1, keepdims=True) + eps)


def _head_pair_sum_matrix(scale=1.0):
    r = lax.broadcasted_iota(jnp.int32, (LANES, LANES), 0) >> _log2(HEAD_DIM)
    c = lax.broadcasted_iota(jnp.int32, (LANES, LANES), 1) >> _log2(HEAD_DIM)
    return jnp.where(r == c, scale, 0.0).astype(BF16)


def _cast_kernel(x_ref, o_ref):
    o_ref[...] = x_ref[...].astype(o_ref.dtype)


def _to_bf16(x):
    flat = x.reshape(-1, x.shape[-1])
    n, w = flat.shape
    tm = _tile(n, max(16, CAST_BLOCK_BYTES // (4 * w)))
    out = pl.pallas_call(
        _cast_kernel,
        grid=(n // tm,),
        in_specs=[pl.BlockSpec((tm, w), lambda i: (i, 0))],
        out_specs=pl.BlockSpec((tm, w), lambda i: (i, 0)),
        out_shape=jax.ShapeDtypeStruct((n, w), BF16),
        compiler_params=_params("parallel"),
        name="to_bf16",
    )(flat)
    return out.reshape(x.shape)


def _ffn_kernel(x_ref, gpre_ref, gpost_ref, wg_ref, wu_ref, wd_ref, o_ref, u_ref, acc_ref):
    f = pl.program_id(1)

    @pl.when(f == 0)
    def _():
        u_ref[...] = (_rms(x_ref[...]) * gpre_ref[...]).astype(BF16)
        acc_ref[...] = jnp.zeros_like(acc_ref)

    u = u_ref[...]
    gate = jnp.dot(u, wg_ref[...], preferred_element_type=F32)
    up = jnp.dot(u, wu_ref[...], preferred_element_type=F32)
    act = (gate * jax.nn.sigmoid(gate) * up).astype(BF16)
    acc_ref[...] += jnp.dot(act, wd_ref[...], preferred_element_type=F32)

    @pl.when(f == pl.num_programs(1) - 1)
    def _():
        o_ref[...] = x_ref[...] + 0.5 * (_rms(acc_ref[...]) * gpost_ref[...])


def _ffn(x, g_pre, g_post, w_gate, w_up, w_down):
    n, d = x.shape
    d_ff = w_gate.shape[1]
    tm = _tile(n, ROW_TILE)
    tf = _tile(d_ff, FF_TILE, LANES)
    return pl.pallas_call(
        _ffn_kernel,
        grid=(n // tm, d_ff // tf),
        in_specs=[
            pl.BlockSpec((tm, d), lambda i, f: (i, 0)),
            pl.BlockSpec((1, d), lambda i, f: (0, 0)),
            pl.BlockSpec((1, d), lambda i, f: (0, 0)),
            pl.BlockSpec((d, tf), lambda i, f: (0, f)),
            pl.BlockSpec((d, tf), lambda i, f: (0, f)),
            pl.BlockSpec((tf, d), lambda i, f: (f, 0)),
        ],
        out_specs=pl.BlockSpec((tm, d), lambda i, f: (i, 0)),
        out_shape=jax.ShapeDtypeStruct((n, d), F32),
        scratch_shapes=[pltpu.VMEM((tm, d), BF16), pltpu.VMEM((tm, d), F32)],
        compiler_params=_params("parallel", "arbitrary"),
        name="ffn_half",
    )(x, g_pre, g_post, w_gate, w_up, w_down)


def _norm_matmul_kernel(x_ref, g_ref, w_ref, o_ref, u_ref):
    @pl.when(pl.program_id(1) == 0)
    def _():
        u_ref[...] = (_rms(x_ref[...]) * g_ref[...]).astype(BF16)

    o_ref[...] = jnp.dot(u_ref[...], w_ref[...], preferred_element_type=F32)


def _norm_matmul(x, g, w, col_tile):
    n, d = x.shape
    cols = w.shape[1]
    tm = _tile(n, ROW_TILE)
    tn = _tile(cols, col_tile, LANES)
    return pl.pallas_call(
        _norm_matmul_kernel,
        grid=(n // tm, cols // tn),
        in_specs=[
            pl.BlockSpec((tm, d), lambda i, j: (i, 0)),
            pl.BlockSpec((1, d), lambda i, j: (0, 0)),
            pl.BlockSpec((d, tn), lambda i, j: (0, j)),
        ],
        out_specs=pl.BlockSpec((tm, tn), lambda i, j: (i, j)),
        out_shape=jax.ShapeDtypeStruct((n, cols), F32),
        scratch_shapes=[pltpu.VMEM((tm, d), BF16)],
        compiler_params=_params("parallel", "arbitrary"),
        name="mix_in",
    )(x, g, w)


def _mix_out_kernel(osb_ref, orw_ref, wsb_ref, wrw_ref, h_ref, g_ref, o_ref):
    o = jnp.dot(osb_ref[...], wsb_ref[...], preferred_element_type=F32)
    o = o + jnp.dot(orw_ref[...], wrw_ref[...], preferred_element_type=F32)
    o_ref[...] = h_ref[...] + _rms(o) * g_ref[...]


def _mix_out(o_sb, o_rw, w_sb, w_rw, h, g):
    n, d = h.shape
    tm = _tile(n, ROW_TILE // 2)
    return pl.pallas_call(
        _mix_out_kernel,
        grid=(n // tm,),
        in_specs=[
            pl.BlockSpec((tm, o_sb.shape[1]), lambda i: (i, 0)),
            pl.BlockSpec((tm, o_rw.shape[1]), lambda i: (i, 0)),
            pl.BlockSpec(w_sb.shape, lambda i: (0, 0)),
            pl.BlockSpec(w_rw.shape, lambda i: (0, 0)),
            pl.BlockSpec((tm, d), lambda i: (i, 0)),
            pl.BlockSpec((1, d), lambda i: (0, 0)),
        ],
        out_specs=pl.BlockSpec((tm, d), lambda i: (i, 0)),
        out_shape=jax.ShapeDtypeStruct((n, d), F32),
        compiler_params=_params("parallel"),
        name="mix_out",
    )(o_sb, o_rw, w_sb, w_rw, h, g)


def _strict_upper(n):
    r = lax.broadcasted_iota(jnp.int32, (n, n), 0)
    c = lax.broadcasted_iota(jnp.int32, (n, n), 1)
    return jnp.where(r > c, 1.0, 0.0).astype(BF16)


def _split_top16(x):
    bits = lax.bitcast_convert_type(x, jnp.uint32) & jnp.uint32(0xFFFF0000)
    hi = lax.bitcast_convert_type(bits, F32)
    return hi.astype(BF16), (x - hi).astype(BF16)


def _sb_sweep(qs, kv_blocks, tri, runs, masks, feature_major=False):
    jobs = []
    for block in kv_blocks:
        for c, (q, mask) in enumerate(zip(qs, masks)):
            k_blk, v_blk = block[c] if isinstance(block, list) else block
            z = _dg(q, k_blk, _NN if feature_major else _NT)
            nz = -z
            soft = jnp.log(1.0 + jnp.exp(jnp.minimum(z, nz)))
            stay = jnp.minimum(nz, 0.0) - soft
            log_sig = stay + z
            if mask is not None:
                stay = jnp.where(mask, stay, 0.0)
            stacked = jnp.concatenate(_split_top16(stay), axis=0)
            jobs.append((c, v_blk, log_sig, stacked, stay[:, :1]))
    between = [jnp.dot(job[3], tri, preferred_element_type=F32) for job in jobs]
    runs = list(runs)
    outs = [None] * len(qs)
    for (c, v_blk, log_sig, _, stay0), btw in zip(jobs, between):
        rows = log_sig.shape[0]
        right = btw[:rows] + btw[rows:]
        a = jnp.exp(log_sig + right)
        if masks[c] is not None:
            a = jnp.where(masks[c], a, 0.0)
        o = jnp.exp(runs[c]) * _dg(a, v_blk, _NT if feature_major else _NN)
        outs[c] = o if outs[c] is None else outs[c] + o
        runs[c] = runs[c] + (right[:, :1] + stay0)
    return outs, runs


def _head_rms(o, gain, first):
    sq = o * o
    s0 = jnp.sum(jnp.where(first, sq, 0.0), axis=-1, keepdims=True)
    s1 = jnp.sum(jnp.where(first, 0.0, sq), axis=-1, keepdims=True)
    ms = jnp.where(first, s0, s1) * (1.0 / HEAD_DIM)
    return o * lax.rsqrt(ms + NORM_EPS) * gain


def _sb_prompt_kernel(q_ref, k_ref, v_ref, tri_ref, gain_ref, o_ref, acc_ref, run_ref, *, blk, group):
    qi = pl.program_id(2)
    first = lax.broadcasted_iota(jnp.int32, (1, LANES), 1) < HEAD_DIM
    lanes = [slice(p * LANES, (p + 1) * LANES) for p in range(group)]
    q_heads = []
    for sl in lanes:
        q = q_ref[0, :, sl] * (HEAD_DIM ** -0.5)
        q_heads += [jnp.where(first, q, 0.0).astype(BF16), jnp.where(first, 0.0, q).astype(BF16)]
    n_chains = len(q_heads)
    tri = tri_ref[...]
    causal = (lax.broadcasted_iota(jnp.int32, (blk, blk), 1)
              < lax.broadcasted_iota(jnp.int32, (blk, blk), 0))

    acc_ref[...] = jnp.zeros_like(acc_ref)
    run_ref[...] = jnp.zeros_like(run_ref)

    def sweep(kb, n_blocks, mask):
        kv = []
        for j in range(n_blocks):
            start = pl.multiple_of((kb - j) * blk, blk)
            k_all = k_ref[0, pl.ds(start, blk), :].astype(BF16)
            v_all = v_ref[0, pl.ds(start, blk), :].astype(BF16)
            kv.append([(k_all[:, lanes[c // 2]], v_all[:, lanes[c // 2]]) for c in range(n_chains)])
        outs, runs = _sb_sweep(q_heads, kv, tri, [run_ref[c] for c in range(n_chains)],
                               [mask] * n_chains)
        for c in range(n_chains):
            run_ref[c] = runs[c]
            acc_ref[c] += outs[c]

    sweep(qi, 1, causal)

    def alive():
        return jnp.max(run_ref[...]) > DEAD_LOG

    rest = qi % ATTN_UNROLL
    singles = jnp.where(rest > 0, rest, jnp.minimum(qi, ATTN_UNROLL))

    def one(carry):
        i, _ = carry
        sweep(qi - 1 - i, 1, None)
        return i + 1, alive()

    _, live = lax.while_loop(lambda c: (c[0] < singles) & c[1], one, (jnp.int32(0), qi >= 0))
    groups = (qi - singles) // ATTN_UNROLL

    def many(carry):
        g, _ = carry
        sweep(qi - 1 - singles - g * ATTN_UNROLL, ATTN_UNROLL, None)
        return g + 1, alive()

    lax.while_loop(lambda c: (c[0] < groups) & c[1], many, (jnp.int32(0), live))
    for p, sl in enumerate(lanes):
        o = jnp.where(first, acc_ref[2 * p], acc_ref[2 * p + 1])
        o_ref[0, :, sl] = _head_rms(o, gain_ref[0, :, sl], first).astype(o_ref.dtype)


def _sb_prompt(p, col0, d_sb, gain, tri):
    b, t, _ = p.shape
    group = ATTN_GROUP
    width = group * LANES
    blk = ATTN_BLOCK
    assert t % blk == 0 and d_sb % width == 0 and col0 % width == 0
    n_groups = d_sb // width
    c0 = col0 // width
    return pl.pallas_call(
        functools.partial(_sb_prompt_kernel, blk=blk, group=group),
        grid=(b, n_groups, t // blk),
        in_specs=[
            pl.BlockSpec((1, blk, width), lambda bi, g, qi: (bi, qi, c0 + g)),
            pl.BlockSpec((1, t, width), lambda bi, g, qi: (bi, 0, c0 + n_groups + g)),
            pl.BlockSpec((1, t, width), lambda bi, g, qi: (bi, 0, c0 + 2 * n_groups + g)),
            pl.BlockSpec((blk, blk), lambda bi, g, qi: (0, 0)),
            pl.BlockSpec((1, 1, width), lambda bi, g, qi: (g, 0, 0)),
        ],
        out_specs=pl.BlockSpec((1, blk, width), lambda bi, g, qi: (bi, qi, g)),
        out_shape=jax.ShapeDtypeStruct((b, t, d_sb), BF16),
        scratch_shapes=[pltpu.VMEM((2 * group, blk, LANES), F32), pltpu.VMEM((2 * group, blk, 1), F32)],
        compiler_params=_params("parallel", "parallel", "arbitrary"),
        name="sb_prompt",
    )(p, p, p, tri, gain.reshape(n_groups, 1, width))


def _sb_sample_kernel(q_ref, kn_ref, vn_ref, kc_hbm, vc_hbm, tri_ref, gain_ref, o_ref,
                      kbuf, vbuf, sem, qs_ref, acc_ref, run_ref, *, n_q, n_heads, layer, blk, n_blocks):
    bi = pl.program_id(0)
    rows = n_heads * n_q
    d_sb = n_heads * HEAD_DIM
    past = n_blocks * blk

    def fetch(i, slot):
        start = pl.multiple_of(past - (i + 1) * blk, blk)
        src = lambda hbm: hbm.at[layer, bi, :, pl.ds(start, blk)]
        return (pltpu.make_async_copy(src(kc_hbm), kbuf.at[slot], sem.at[slot, 0]),
                pltpu.make_async_copy(src(vc_hbm), vbuf.at[slot], sem.at[slot, 1]))

    for cp in fetch(0, 0):
        cp.start()

    row_head = lax.broadcasted_iota(jnp.int32, (rows, 1), 0) >> _log2(n_q)
    lane_head = lax.broadcasted_iota(jnp.int32, (1, d_sb), 1) >> _log2(HEAD_DIM)
    q = jnp.tile(q_ref[0] * (HEAD_DIM ** -0.5), (n_heads, 1))
    qs = jnp.where(row_head == lane_head, q, 0.0).astype(BF16)
    qs_ref[...] = qs
    n_new = kn_ref.shape[1]
    q_pos = lax.broadcasted_iota(jnp.int32, (rows, 1), 0) & (n_q - 1)
    mask = lax.broadcasted_iota(jnp.int32, (1, n_new), 1) < q_pos
    outs, runs = _sb_sweep([qs], [(kn_ref[0].astype(BF16), vn_ref[0].astype(BF16))],
                           tri_ref[:n_new, :n_new], [jnp.zeros((rows, 1), F32)], [mask])
    run_ref[...] = runs[0]
    acc_ref[...] = outs[0]

    def alive():
        return jnp.max(run_ref[...]) > DEAD_LOG

    half = rows // 2
    chains = [slice(0, half), slice(half, rows)]
    sub = tri_ref.shape[0]

    def body(carry):
        i, _ = carry
        slot = i & 1
        for cp in fetch(i, slot):
            cp.wait()

        @pl.when(i + 1 < n_blocks)
        def _():
            for cp in fetch(i + 1, 1 - slot):
                cp.start()

        kv = [(kbuf[slot, :, s:s + sub].astype(BF16), vbuf[slot, :, s:s + sub].astype(BF16))
              for s in range(blk - sub, -1, -sub)]
        outs, runs = _sb_sweep([qs_ref[rs, :] for rs in chains], kv, tri_ref[...],
                               [run_ref[rs, :] for rs in chains], [None, None], feature_major=True)
        for rs, o, run in zip(chains, outs, runs):
            run_ref[rs, :] = run
            acc_ref[rs, :] += o
        return i + 1, alive()

    done, _ = lax.while_loop(lambda c: (c[0] < n_blocks) & c[1], body, (jnp.int32(0), alive()))

    @pl.when(done < n_blocks)
    def _():
        for cp in fetch(done, done & 1):
            cp.wait()

    o = jnp.zeros((n_q, d_sb), F32)
    for h in range(n_heads):
        o = o + jnp.where(lane_head == h, acc_ref[h * n_q:(h + 1) * n_q, :], 0.0)
    first = (lax.broadcasted_iota(jnp.int32, (1, LANES), 1) < HEAD_DIM)
    for p in range(d_sb // LANES):
        sl = slice(p * LANES, (p + 1) * LANES)
        o_ref[0, :, sl] = _head_rms(o[:, sl], gain_ref[:, sl], first).astype(o_ref.dtype)


def _sb_sample(p_sb, cache_k, cache_v, layer, gain, tri):
    b, n_q, w = p_sb.shape
    d_sb = w // 3
    n_heads = d_sb // HEAD_DIM
    past = cache_k.shape[3]
    blk = min(CACHE_BLOCK, past)
    assert past % blk == 0 and blk % tri.shape[0] == 0 and n_q <= LANES
    n_blocks = past // blk
    pad = ((0, 0), (0, LANES - n_q), (0, 0))
    k_new = jnp.pad(p_sb[:, :, d_sb:2 * d_sb], pad)
    v_new = jnp.pad(p_sb[:, :, 2 * d_sb:], pad)
    rows = n_heads * n_q
    return pl.pallas_call(
        functools.partial(_sb_sample_kernel, n_q=n_q, n_heads=n_heads, layer=layer, blk=blk,
                          n_blocks=n_blocks),
        grid=(b,),
        in_specs=[
            pl.BlockSpec((1, n_q, d_sb), lambda bi: (bi, 0, 0)),
            pl.BlockSpec((1, LANES, d_sb), lambda bi: (bi, 0, 0)),
            pl.BlockSpec((1, LANES, d_sb), lambda bi: (bi, 0, 0)),
            pl.BlockSpec(memory_space=pl.ANY),
            pl.BlockSpec(memory_space=pl.ANY),
            pl.BlockSpec(tri.shape, lambda bi: (0, 0)),
            pl.BlockSpec((1, d_sb), lambda bi: (0, 0)),
        ],
        out_specs=pl.BlockSpec((1, n_q, d_sb), lambda bi: (bi, 0, 0)),
        out_shape=jax.ShapeDtypeStruct((b, n_q, d_sb), BF16),
        scratch_shapes=[pltpu.VMEM((2, d_sb, blk), cache_k.dtype), pltpu.VMEM((2, d_sb, blk), cache_v.dtype),
                        pltpu.SemaphoreType.DMA((2, 2)),
                        pltpu.VMEM((rows, d_sb), BF16), pltpu.VMEM((rows, d_sb), F32),
                        pltpu.VMEM((rows, 1), F32)],
        compiler_params=_params("arbitrary"),
        name="sb_sample",
    )(p_sb, k_new, v_new, cache_k, cache_v, tri, gain)


def _stack(x, first):
    return jnp.concatenate([jnp.where(first, x, 0.0), jnp.where(first, 0.0, x)], axis=0)


def _each(f, *cols):
    return [f(*xs) for xs in zip(*cols)]


def _sum_sel(x, sel, pieces):
    parts = _split2(x) if pieces == 2 else _split3(x)
    n = x.shape[0]
    out = _dg(jnp.concatenate(parts, axis=0), sel, _NN)
    return sum(out[i * n:(i + 1) * n] for i in range(1, len(parts))) + out[:n]


def _unit_lower_inverses(lows, c):
    n = lows[0].shape[0]
    r = lax.broadcasted_iota(jnp.int32, (n, n), 0)
    col = lax.broadcasted_iota(jnp.int32, (n, n), 1)
    eye = jnp.where(r == col, 1.0, 0.0)
    invs = [eye for _ in lows]
    for lvl in range(_log2(c)):
        same = (r >> (lvl + 1)) == (col >> (lvl + 1))
        lower_left = same & (((r >> lvl) & 1) == 1) & (((col >> lvl) & 1) == 0)
        offs = [jnp.where(lower_left, low, 0.0) for low in lows]
        if lvl == 0:
            invs = [eye + off for off in offs]
            continue
        left = _each(lambda inv, off: _dg(inv, off, _NN), invs, offs)
        invs = _each(lambda inv, x: inv + _dg(x, inv, _NN), invs, left)
    return invs


def _rwkv_mix_kernel(p_ref, shift_ref, mu_ref, w0_ref, w2_ref, a0_ref, a2_ref, g2_ref,
                     kk_ref, ka_ref, rk_ref, lnw_ref, lnb_ref, s0_ref, o_ref, sT_ref, s_ref, prev_ref,
                     *, c, t_valid, d_rw, wd, wa):
    ci = pl.program_id(1)

    @pl.when(ci == 0)
    def _():
        s_ref[...] = s0_ref[0]
        prev_ref[...] = shift_ref[0]

    @pl.when(ci * c < t_valid)
    def _():
        p = p_ref[0]
        row = lax.broadcasted_iota(jnp.int32, (c, 1), 0)
        prev = jnp.where(row == 0, prev_ref[...], pltpu.roll(p, 1, 0))
        prev_ref[...] = p[c - 1:c, :]
        xs = p + (prev - p) * mu_ref[...]
        o3 = 3 * d_rw
        dw = xs[:, o3:o3 + wd]
        da = xs[:, o3 + wd:o3 + wd + wa]
        dg = xs[:, o3 + wd + wa:]
        w_pre = w0_ref[...] + _dg(jnp.tanh(dw), w2_ref[...], _NN)
        w_log = -(jnp.maximum(-w_pre, 0.0) + jnp.log1p(jnp.exp(-jnp.abs(w_pre)))) - 0.5
        streams = (xs[:, :d_rw], xs[:, d_rw:2 * d_rw], xs[:, 2 * d_rw:o3],
                   -jnp.exp(w_log),
                   jax.nn.sigmoid(a0_ref[...] + _dg(da, a2_ref[...], _NN)),
                   _dg(jax.nn.sigmoid(dg), g2_ref[...], _NN))
        _rwkv_scan_chunk(ci, streams, kk_ref, ka_ref, rk_ref, lnw_ref, lnb_ref, o_ref, s_ref,
                         c=c, t_valid=t_valid, group=d_rw // LANES)

    @pl.when(ci * c >= t_valid)
    def _():
        o_ref[...] = jnp.zeros_like(o_ref)

    @pl.when(ci == pl.num_programs(1) - 1)
    def _():
        sT_ref[0] = s_ref[...]


def _rwkv_scan_chunk(ci, streams, kk_ref, ka_ref, rk_ref, lnw_ref, lnb_ref, o_ref, s_ref, *,
                     c, t_valid, group):
    lanes = [slice(p * LANES, (p + 1) * LANES) for p in range(group)]
    take = lambda x: [x[:, sl] for sl in lanes]
    par = lambda ref: [ref[:, sl] for sl in lanes]
    r_all, k_all, v_all, lw_all, al_all, g_all = streams
    n = 2 * c
    first = lax.broadcasted_iota(jnp.int32, (1, LANES), 1) < HEAD_DIM
    valid = (ci * c + lax.broadcasted_iota(jnp.int32, (c, 1), 0)) < t_valid
    pair_sum = _head_pair_sum_matrix()
    pair_mean = _head_pair_sum_matrix(1.0 / HEAD_DIM)
    t_i = lax.broadcasted_iota(jnp.int32, (c, c), 0)
    s_i = lax.broadcasted_iota(jnp.int32, (c, c), 1)
    upto = jnp.where(s_i <= t_i, 1.0, 0.0).astype(BF16)
    tt = lax.broadcasted_iota(jnp.int32, (n, n), 0) & (c - 1)
    ss = lax.broadcasted_iota(jnp.int32, (n, n), 1) & (c - 1)
    strict, incl = ss < tt, ss <= tt
    diag = (lax.broadcasted_iota(jnp.int32, (LANES, LANES), 0)
            == lax.broadcasted_iota(jnp.int32, (LANES, LANES), 1))

    r = take(r_all)
    k = [jnp.where(valid, x, 0.0) for x in take(k_all)]
    v = [jnp.where(valid, x, 0.0) for x in take(v_all)]
    lw = [jnp.where(valid, x, 0.0) for x in take(lw_all)]
    al = take(al_all)
    g = take(g_all)

    kk = _each(lambda x, w: x * w, k, par(kk_ref))
    norm = [jnp.sqrt(_sum_sel(x * x, pair_sum, 2)) for x in kk]
    kk = _each(lambda x, nr: x / jnp.maximum(nr, KK_EPS), kk, norm)
    k_mod = _each(lambda x, a_, w: x * (1.0 + (a_ - 1.0) * w), k, al, par(ka_ref))
    b = _each(lambda x, a_: x * a_, kk, al)

    def cumsum(x):
        out = _dg(upto, jnp.concatenate(_split3(x), axis=1), _NN)
        return out[:, :LANES] + out[:, LANES:2 * LANES] + out[:, 2 * LANES:]

    cum = [cumsum(x) for x in lw]
    cum_end = [x[c - 1:c, :] for x in cum]
    dec_out = [jnp.exp(-x) for x in cum]
    dec_rest = _each(lambda e, x: jnp.exp(e - x), cum_end, cum)
    a_t = _each(lambda x, cu, l: _stack(-x * jnp.exp(cu - l), first), kk, cum, lw)
    r_t = _each(lambda x, cu: _stack(x * jnp.exp(cu), first), r, cum)
    b_t = _each(lambda x, d: _stack(x * d, first), b, dec_out)
    k_t = _each(lambda x, d: _stack(x * d, first), k_mod, dec_out)
    b_h = _each(lambda x, d: _stack(x * d, first), b, dec_rest)
    k_h = _each(lambda x, d: _stack(x * d, first), k_mod, dec_rest)
    v_s = [_stack(x, first) for x in v]

    prod = _each(lambda a_, r_, b_, k_: _dg(jnp.concatenate([a_, r_], axis=0),
                                            jnp.concatenate([b_, k_], axis=0), _NT), a_t, r_t, b_t, k_t)
    l_ab = [jnp.where(strict, x[:n, :n], 0.0) for x in prod]
    l_ak = [jnp.where(strict, x[:n, n:], 0.0) for x in prod]
    m_r = [jnp.where(jnp.concatenate([incl, incl], axis=1), x[n:], 0.0) for x in prod]

    solve = _unit_lower_inverses(l_ab, c)
    lv = _each(lambda l_, v_: _dg(l_, v_, _NN), l_ak, v_s)
    au = _each(lambda t_, a_, x: _dg(t_, jnp.concatenate([a_, x], axis=1), _NN), solve, a_t, lv)
    rhs = _each(lambda x, v_: jnp.concatenate(
        [x, jnp.concatenate([jnp.zeros_like(v_), v_], axis=1)], axis=0), au, v_s)
    ry = _each(lambda m, x: _dg(m, x, _NN), m_r, rhs)
    gh = _each(lambda b_, k_, x: _dg(jnp.concatenate([b_, k_], axis=0), x, _TN), b_h, k_h, rhs)

    lhs = _each(lambda r_, x, y, e: jnp.concatenate(
        [r_ + x[:, :LANES], jnp.where(diag, jnp.exp(e), 0.0) + y[:, :LANES]], axis=0), r_t, ry, gh, cum_end)
    step = [_dg3(x, s_ref[p]) for p, x in enumerate(lhs)]
    for p in range(group):
        s_ref[p] = step[p][n:] + gh[p][:, LANES:]
    y = _each(lambda st, x: st[:c] + st[c:n] + x[:c, LANES:] + x[c:, LANES:], step, ry)

    mean = [_sum_sel(x, pair_mean, 2) for x in y]
    d = _each(lambda x, m: x - m, y, mean)
    var = [_sum_sel(x * x, pair_mean, 2) for x in d]
    bonus = _each(lambda r_, k_, w: _sum_sel(r_ * k_ * w, pair_sum, 2), r, k_mod, par(rk_ref))
    for p, sl in enumerate(lanes):
        yn = d[p] * lax.rsqrt(var[p] + GN_EPS) * lnw_ref[:, sl] + lnb_ref[:, sl]
        o_ref[0, :, sl] = ((yn + bonus[p] * v[p]) * g[p]).astype(o_ref.dtype)


def _rwkv_mix(p, pw, shift0, mu, w0, w2, a0, a2, g2, k_k, k_a, r_k, ln_w, ln_b, s0, t_valid):
    b, t_in, _ = p.shape
    d_rw = w0.shape[1]
    pairs = d_rw // LANES
    c = SCAN_CHUNK
    t = -(-t_in // c) * c
    if t != t_in:
        p = jnp.pad(p, ((0, 0), (0, t - t_in), (0, 0)))
    full = lambda a: pl.BlockSpec(a.shape, lambda bi, ci: (0,) * a.ndim)
    st_spec = pl.BlockSpec((1, pairs, LANES, LANES), lambda bi, ci: (bi, 0, 0, 0))
    params = (mu, w0, w2, a0, a2, g2, k_k, k_a, r_k, ln_w, ln_b)
    o, s_t = pl.pallas_call(
        functools.partial(_rwkv_mix_kernel, c=c, t_valid=t_valid, d_rw=d_rw, wd=w2.shape[0],
                          wa=a2.shape[0]),
        grid=(b, t // c),
        in_specs=[pl.BlockSpec((1, c, pw), lambda bi, ci: (bi, ci, 0)),
                  pl.BlockSpec((1, 1, pw), lambda bi, ci: (bi, 0, 0))]
                 + [full(a) for a in params] + [st_spec],
        out_specs=[pl.BlockSpec((1, c, d_rw), lambda bi, ci: (bi, ci, 0)), st_spec],
        out_shape=[jax.ShapeDtypeStruct((b, t, d_rw), BF16),
                   jax.ShapeDtypeStruct((b, pairs, LANES, LANES), F32)],
        scratch_shapes=[pltpu.VMEM((pairs, LANES, LANES), F32), pltpu.VMEM((1, pw), F32)],
        compiler_params=_params("parallel", "arbitrary"),
        name="rwkv_mix",
    )(p, shift0, *params, s0)
    return o[:, :t_in], s_t


def _state_to_pairs(s):
    b, h, e, _ = s.shape
    st = jnp.swapaxes(s, -1, -2).reshape(b, h // 2, 2, e, e)
    z = jnp.zeros_like(st[:, :, 0])
    top = jnp.concatenate([st[:, :, 0], z], axis=-1)
    bot = jnp.concatenate([z, st[:, :, 1]], axis=-1)
    return jnp.concatenate([top, bot], axis=-2)


def _pairs_to_state(sp):
    b, pairs = sp.shape[:2]
    e = HEAD_DIM
    blocks = jnp.stack([sp[:, :, :e, :e], sp[:, :, e:, e:]], axis=2)
    return jnp.swapaxes(blocks, -1, -2).reshape(b, 2 * pairs, e, e)


def _pad_cols(x, width):
    return jnp.pad(x, [(0, 0)] * (x.ndim - 1) + [(0, width - x.shape[-1])])


def _rw_layout(x, d_rw, ranks, widths):
    parts = [x[..., :3 * d_rw]]
    o = 3 * d_rw
    for rank, width in zip(ranks, widths):
        parts.append(_pad_cols(x[..., o:o + rank], width))
        o += rank
    return jnp.concatenate(parts, axis=-1)


def _rw_unlayout(x, d_rw, ranks, widths):
    parts = [x[..., :3 * d_rw]]
    o = 3 * d_rw
    for rank, width in zip(ranks, widths):
        parts.append(x[..., o:o + rank])
        o += width
    return jnp.concatenate(parts, axis=-1)


def _layer(h, lw, cache, s0_pairs, shift0, t_valid, tri):
    b, t, d = h.shape
    d_sb = lw["sb_gain"].shape[-1]
    x = h.reshape(b * t, d)
    x = _ffn(x, lw["ffn1_pre"], lw["ffn1_post"], lw["ffn1_gate"], lw["ffn1_up"], lw["ffn1_down"])
    p = _norm_matmul(x, lw["mix_pre"], lw["w_in"], MIX_IN_COLS).reshape(b, t, -1)
    rw_cols = p.shape[-1] - 3 * d_sb
    if cache is None:
        o_sb = _sb_prompt(p, rw_cols, d_sb, lw["sb_gain"], tri)
    else:
        o_sb = _sb_sample(p[:, :, rw_cols:], cache[0], cache[1], cache[2], lw["sb_gain"], tri)
    o_rw, s_t = _rwkv_mix(p, rw_cols, shift0, lw["mu"], lw["w0"], lw["w2"], lw["a0"], lw["a2"], lw["g2"],
                          lw["k_k"], lw["k_a"], lw["r_k"], lw["ln_w"], lw["ln_b"], s0_pairs, t_valid)
    x = _mix_out(o_sb.reshape(b * t, -1), o_rw.reshape(b * t, -1), lw["w_out_sb"], lw["w_out_rw"],
                 x, lw["mix_post"])
    x = _ffn(x, lw["ffn2_pre"], lw["ffn2_post"], lw["ffn2_gate"], lw["ffn2_up"], lw["ffn2_down"])
    k_new = p[:, :t_valid, rw_cols + d_sb:rw_cols + 2 * d_sb]
    v_new = p[:, :t_valid, rw_cols + 2 * d_sb:]
    return x.reshape(b, t, d), k_new, v_new, s_t, p[:, t_valid - 1, :rw_cols]


def kernel(x_prompt, x_sample, cache_sb_k, cache_sb_v, state_rwkv_S, state_rwkv_shift, meta_tokens, ffn1_norm_pre, ffn1_norm_post, ffn1_w_gate, ffn1_w_up, ffn1_w_down, mix_norm_pre, mix_norm_post, w_in, sb_out_gain, rwkv_mu, rwkv_w0, rwkv_w2, rwkv_a0, rwkv_a2, rwkv_g2, rwkv_k_k, rwkv_k_a, rwkv_r_k, rwkv_ln_w, rwkv_ln_b, w_out, ffn2_norm_pre, ffn2_norm_post, ffn2_w_gate, ffn2_w_up, ffn2_w_down):
    depth, d = ffn1_norm_pre.shape
    bp, seq, _ = x_prompt.shape
    bs, dec_seq, _ = x_sample.shape
    n_meta = meta_tokens.shape[0]
    h_sb = sb_out_gain.shape[1]
    d_sb = h_sb * HEAD_DIM
    d_rw = rwkv_w0.shape[1]
    h_rw = d_rw // HEAD_DIM
    ranks = (rwkv_w2.shape[1], rwkv_a2.shape[1], rwkv_g2.shape[1])
    widths = tuple(-(-r // LANES) * LANES for r in ranks)
    p_rw_cols = 3 * d_rw + sum(ranks)
    past = cache_sb_k.shape[2]

    t_valid = n_meta + seq
    t_pad = -(-t_valid // ATTN_BLOCK) * ATTN_BLOCK
    meta = jnp.broadcast_to(meta_tokens[None].astype(x_prompt.dtype), (bp, n_meta, d))
    hp = jnp.concatenate([meta, x_prompt, jnp.zeros((bp, t_pad - t_valid, d), x_prompt.dtype)], axis=1)
    hs = x_sample

    tri = _strict_upper(ATTN_BLOCK)
    feature_major = lambda c: jnp.transpose(c, (0, 1, 3, 4, 2)).reshape(depth, bs, d_sb, past)
    cache_kt, cache_vt = feature_major(cache_sb_k), feature_major(cache_sb_v)
    row = lambda x: x.reshape(1, -1)
    pad_rows = lambda w, width: jnp.pad(w, ((0, width - w.shape[0]), (0, 0)))
    w_in_rw = _rw_layout(w_in[:, :, 3 * d_sb:], d_rw, ranks, widths)
    w_in_all = jnp.concatenate([w_in_rw, w_in[:, :, :3 * d_sb]], axis=-1).astype(BF16)
    ffn_w = [_to_bf16(w) for w in (ffn1_w_gate, ffn1_w_up, ffn1_w_down, ffn2_w_gate, ffn2_w_up, ffn2_w_down)]
    mu_all = _rw_layout(rwkv_mu, d_rw, ranks, widths)

    outs = [[] for _ in range(8)]
    s0_p = jnp.zeros((bp, h_rw // 2, LANES, LANES), F32)
    shift0_p = jnp.zeros((bp, 1, w_in_rw.shape[-1]), F32)
    for l in range(depth):
        lw = dict(
            ffn1_pre=row(ffn1_norm_pre[l]), ffn1_post=row(ffn1_norm_post[l]),
            ffn1_gate=ffn_w[0][l], ffn1_up=ffn_w[1][l], ffn1_down=ffn_w[2][l],
            ffn2_pre=row(ffn2_norm_pre[l]), ffn2_post=row(ffn2_norm_post[l]),
            ffn2_gate=ffn_w[3][l], ffn2_up=ffn_w[4][l], ffn2_down=ffn_w[5][l],
            mix_pre=row(mix_norm_pre[l]), mix_post=row(mix_norm_post[l]),
            w_in=w_in_all[l], sb_gain=row(sb_out_gain[l]),
            mu=row(mu_all[l]), w0=row(rwkv_w0[l]), a0=row(rwkv_a0[l]),
            w2=pad_rows(rwkv_w2[l], widths[0]).astype(BF16),
            a2=pad_rows(rwkv_a2[l], widths[1]).astype(BF16),
            g2=pad_rows(rwkv_g2[l], widths[2]).astype(BF16),
            k_k=row(rwkv_k_k[l]), k_a=row(rwkv_k_a[l]), r_k=row(rwkv_r_k[l]),
            ln_w=row(rwkv_ln_w[l]), ln_b=row(rwkv_ln_b[l]),
            w_out_sb=w_out[l, :d_sb].astype(BF16), w_out_rw=w_out[l, d_sb:].astype(BF16),
        )
        hp, kp, vp, sp, shp = _layer(hp, lw, None, s0_p, shift0_p, t_valid, tri)
        cache = (cache_kt, cache_vt, l)
        shift0_s = _rw_layout(state_rwkv_shift[l], d_rw, ranks, widths)[:, None, :]
        hs, ks, vs, ss, shs = _layer(hs, lw, cache, _state_to_pairs(state_rwkv_S[l].astype(F32)),
                                     shift0_s, dec_seq, tri)
        per_layer = (
            kp.reshape(bp, t_valid, h_sb, HEAD_DIM), vp.reshape(bp, t_valid, h_sb, HEAD_DIM),
            _pairs_to_state(sp).astype(state_rwkv_S.dtype),
            _rw_unlayout(shp, d_rw, ranks, widths).astype(state_rwkv_shift.dtype),
            ks.reshape(bs, dec_seq, h_sb, HEAD_DIM), vs.reshape(bs, dec_seq, h_sb, HEAD_DIM),
            _pairs_to_state(ss).astype(state_rwkv_S.dtype),
            _rw_unlayout(shs, d_rw, ranks, widths).astype(state_rwkv_shift.dtype),
        )
        for acc, val in zip(outs, per_layer):
            acc.append(val)
    assert p_rw_cols == state_rwkv_shift.shape[-1]
    y_prompt = hp[:, n_meta:t_valid]
    return (y_prompt, hs) + tuple(jnp.stack(o) for o in outs)
```

```python
import functools

import jax
import jax.numpy as jnp
from jax import lax
from jax.experimental import pallas as pl
from jax.experimental.pallas import tpu as pltpu

F32 = jnp.float32
BF16 = jnp.bfloat16

HEAD_DIM = 64
LANES = 128
NORM_EPS = 1e-6
GN_EPS = 64e-5
KK_EPS = 1e-12
VMEM_LIMIT_BYTES = 56 * 2 ** 20

LOG2E = 1.4426950408889634
ATTN_BLOCK = 256
ATTN_GROUP = 2
MIX_IN_COLS = 1664
DEAD_LOG = -104.0
ATTN_UNROLL = 2
CACHE_BLOCK = 512
SCAN_CHUNK = 64
ROW_TILE = 512
FF_TILE = 512
CAST_BLOCK_BYTES = 8 * 2 ** 20


def _params(*sem):
    return pltpu.CompilerParams(dimension_semantics=sem, vmem_limit_bytes=VMEM_LIMIT_BYTES)


def _tile(n, pref, mult=16):
    t = min(pref, n)
    t -= t % mult
    while t > mult and n % t:
        t -= mult
    assert t >= mult and n % t == 0, (n, pref)
    return t


def _log2(n):
    k = n.bit_length() - 1
    assert 1 << k == n, n
    return k


_NN = (((1,), (0,)), ((), ()))
_NT = (((1,), (1,)), ((), ()))
_TN = (((0,), (0,)), ((), ()))


def _dg(a, b, dims):
    return lax.dot_general(a.astype(BF16), b.astype(BF16), dims, preferred_element_type=F32)


def _split2(x):
    hi = x.astype(BF16)
    lo = (x - hi.astype(F32)).astype(BF16)
    return hi, lo


def _split3(x):
    hi = x.astype(BF16)
    r = x - hi.astype(F32)
    mid = r.astype(BF16)
    lo = (r - mid.astype(F32)).astype(BF16)
    return hi, mid, lo


def _dg3(a, b, dims=_NN):
    ah, al = _split2(a)
    bh, bl = _split2(b)
    return _dg(ah, bh, dims) + (_dg(ah, bl, dims) + _dg(al, bh, dims))


def _dg_sel(x, sel, pieces):
    parts = _split2(x) if pieces == 2 else _split3(x)
    out = _dg(parts[0], sel, _NN)
    for p in parts[1:]:
        out = out + _dg(p, sel, _NN)
    return out


def _sel_dg(sel, x, pieces):
    parts = _split2(x) if pieces == 2 else _split3(x)
    out = _dg(sel, parts[0], _NN)
    for p in parts[1:]:
        out = out + _dg(sel, p, _NN)
    return out


def _rms(x, eps=NORM_EPS):
    return x * lax.rsqrt(jnp.mean(x * x, axis=-1, keepdims=True) + eps)


def _head_pair_sum_matrix(scale=1.0):
    r = lax.broadcasted_iota(jnp.int32, (LANES, LANES), 0) >> _log2(HEAD_DIM)
    c = lax.broadcasted_iota(jnp.int32, (LANES, LANES), 1) >> _log2(HEAD_DIM)
    return jnp.where(r == c, scale, 0.0).astype(BF16)


def _cast_kernel(x_ref, o_ref):
    o_ref[...] = x_ref[...].astype(o_ref.dtype)


def _to_bf16(x):
    flat = x.reshape(-1, x.shape[-1])
    n, w = flat.shape
    tm = _tile(n, max(16, CAST_BLOCK_BYTES // (4 * w)))
    out = pl.pallas_call(
        _cast_kernel,
        grid=(n // tm,),
        in_specs=[pl.BlockSpec((tm, w), lambda i: (i, 0))],
        out_specs=pl.BlockSpec((tm, w), lambda i: (i, 0)),
        out_shape=jax.ShapeDtypeStruct((n, w), BF16),
        compiler_params=_params("parallel"),
        name="to_bf16",
    )(flat)
    return out.reshape(x.shape)


def _ffn_kernel(x_ref, gpre_ref, gpost_ref, wg_ref, wu_ref, wd_ref, o_ref, u_ref, acc_ref):
    f = pl.program_id(1)

    @pl.when(f == 0)
    def _():
        u_ref[...] = (_rms(x_ref[...]) * gpre_ref[...]).astype(BF16)
        acc_ref[...] = jnp.zeros_like(acc_ref)

    u = u_ref[...]
    gate = jnp.dot(u, wg_ref[...], preferred_element_type=F32)
    up = jnp.dot(u, wu_ref[...], preferred_element_type=F32)
    act = (gate * jax.nn.sigmoid(gate) * up).astype(BF16)
    acc_ref[...] += jnp.dot(act, wd_ref[...], preferred_element_type=F32)

    @pl.when(f == pl.num_programs(1) - 1)
    def _():
        o_ref[...] = x_ref[...] + 0.5 * (_rms(acc_ref[...]) * gpost_ref[...])


def _ffn(x, g_pre, g_post, w_gate, w_up, w_down):
    n, d = x.shape
    d_ff = w_gate.shape[1]
    tm = _tile(n, ROW_TILE)
    tf = _tile(d_ff, FF_TILE, LANES)
    return pl.pallas_call(
        _ffn_kernel,
        grid=(n // tm, d_ff // tf),
        in_specs=[
            pl.BlockSpec((tm, d), lambda i, f: (i, 0)),
            pl.BlockSpec((1, d), lambda i, f: (0, 0)),
            pl.BlockSpec((1, d), lambda i, f: (0, 0)),
            pl.BlockSpec((d, tf), lambda i, f: (0, f)),
            pl.BlockSpec((d, tf), lambda i, f: (0, f)),
            pl.BlockSpec((tf, d), lambda i, f: (f, 0)),
        ],
        out_specs=pl.BlockSpec((tm, d), lambda i, f: (i, 0)),
        out_shape=jax.ShapeDtypeStruct((n, d), F32),
        scratch_shapes=[pltpu.VMEM((tm, d), BF16), pltpu.VMEM((tm, d), F32)],
        compiler_params=_params("parallel", "arbitrary"),
        name="ffn_half",
    )(x, g_pre, g_post, w_gate, w_up, w_down)


def _norm_matmul_kernel(x_ref, g_ref, w_ref, o_ref):
    u = (_rms(x_ref[...]) * g_ref[...]).astype(BF16)
    o_ref[...] = jnp.dot(u, w_ref[...], preferred_element_type=F32)


def _norm_matmul(x, g, w, col_tile):
    n, d = x.shape
    cols = w.shape[1]
    tm = _tile(n, ROW_TILE)
    tn = _tile(cols, col_tile, LANES)
    return pl.pallas_call(
        _norm_matmul_kernel,
        grid=(cols // tn, n // tm),
        in_specs=[
            pl.BlockSpec((tm, d), lambda j, i: (i, 0)),
            pl.BlockSpec((1, d), lambda j, i: (0, 0)),
            pl.BlockSpec((d, tn), lambda j, i: (0, j)),
        ],
        out_specs=pl.BlockSpec((tm, tn), lambda j, i: (i, j)),
        out_shape=jax.ShapeDtypeStruct((n, cols), F32),
        compiler_params=_params("parallel", "parallel"),
        name="mix_in",
    )(x, g, w)


def _mix_out_kernel(osb_ref, orw_ref, wsb_ref, wrw_ref, h_ref, g_ref, o_ref):
    o = jnp.dot(osb_ref[...], wsb_ref[...], preferred_element_type=F32)
    o = o + jnp.dot(orw_ref[...], wrw_ref[...], preferred_element_type=F32)
    o_ref[...] = h_ref[...] + _rms(o) * g_ref[...]


def _mix_out(o_sb, o_rw, w_sb, w_rw, h, g):
    n, d = h.shape
    tm = _tile(n, ROW_TILE // 2)
    return pl.pallas_call(
        _mix_out_kernel,
        grid=(n // tm,),
        in_specs=[
            pl.BlockSpec((tm, o_sb.shape[1]), lambda i: (i, 0)),
            pl.BlockSpec((tm, o_rw.shape[1]), lambda i: (i, 0)),
            pl.BlockSpec(w_sb.shape, lambda i: (0, 0)),
            pl.BlockSpec(w_rw.shape, lambda i: (0, 0)),
            pl.BlockSpec((tm, d), lambda i: (i, 0)),
            pl.BlockSpec((1, d), lambda i: (0, 0)),
        ],
        out_specs=pl.BlockSpec((tm, d), lambda i: (i, 0)),
        out_shape=jax.ShapeDtypeStruct((n, d), F32),
        compiler_params=_params("parallel"),
        name="mix_out",
    )(o_sb, o_rw, w_sb, w_rw, h, g)


def _strict_upper(n):
    r = lax.broadcasted_iota(jnp.int32, (n, n), 0)
    c = lax.broadcasted_iota(jnp.int32, (n, n), 1)
    return jnp.where(r > c, 1.0, 0.0).astype(BF16)


def _split_top16(x):
    bits = lax.bitcast_convert_type(x, jnp.uint32) & jnp.uint32(0xFFFF0000)
    hi = lax.bitcast_convert_type(bits, F32)
    return hi.astype(BF16), (x - hi).astype(BF16)


def _sb_sweep(qs, kv_blocks, tri, runs, masks, feature_major=False, masked_blocks=None):
    masked_blocks = len(kv_blocks) if masked_blocks is None else masked_blocks
    jobs = []
    for j, block in enumerate(kv_blocks):
        for c, q in enumerate(qs):
            mask = masks[c] if j < masked_blocks else None
            k_blk, v_blk = block[c] if isinstance(block, list) else block
            z = _dg(q, k_blk, _NN if feature_major else _NT)
            nz = -z
            soft = jnp.log(1.0 + jnp.exp(jnp.minimum(z, nz)))
            stay = jnp.minimum(nz, 0.0) - soft
            log_sig = stay + z
            if mask is not None:
                stay = jnp.where(mask, stay, 0.0)
            stacked = jnp.concatenate(_split_top16(stay), axis=0)
            jobs.append((c, v_blk, log_sig, stacked, stay[:, :1], mask))
    between = [jnp.dot(job[3], tri, preferred_element_type=F32) for job in jobs]
    runs = list(runs)
    outs = [None] * len(qs)
    for (c, v_blk, log_sig, _, stay0, mask), btw in zip(jobs, between):
        rows = log_sig.shape[0]
        right = btw[:rows] + btw[rows:]
        a = jnp.exp(log_sig + right)
        if mask is not None:
            a = jnp.where(mask, a, 0.0)
        o = jnp.exp(runs[c]) * _dg(a, v_blk, _NT if feature_major else _NN)
        outs[c] = o if outs[c] is None else outs[c] + o
        runs[c] = runs[c] + (right[:, :1] + stay0)
    return outs, runs


def _head_rms(o, gain, first):
    sq = o * o
    s0 = jnp.sum(jnp.where(first, sq, 0.0), axis=-1, keepdims=True)
    s1 = jnp.sum(jnp.where(first, 0.0, sq), axis=-1, keepdims=True)
    ms = jnp.where(first, s0, s1) * (1.0 / HEAD_DIM)
    return o * lax.rsqrt(ms + NORM_EPS) * gain


def _sb_prompt_kernel(q_ref, k_ref, v_ref, tri_ref, gain_ref, o_ref, acc_ref, run_ref, *, blk, group):
    qi = pl.program_id(2)
    first = lax.broadcasted_iota(jnp.int32, (1, LANES), 1) < HEAD_DIM
    lanes = [slice(p * LANES, (p + 1) * LANES) for p in range(group)]
    q_heads = []
    for sl in lanes:
        q = q_ref[0, :, sl] * (HEAD_DIM ** -0.5)
        q_heads += [jnp.where(first, q, 0.0).astype(BF16), jnp.where(first, 0.0, q).astype(BF16)]
    n_chains = len(q_heads)
    tri = tri_ref[...]
    causal = (lax.broadcasted_iota(jnp.int32, (blk, blk), 1)
              < lax.broadcasted_iota(jnp.int32, (blk, blk), 0))

    acc_ref[...] = jnp.zeros_like(acc_ref)
    run_ref[...] = jnp.zeros_like(run_ref)

    def sweep(kb, n_blocks, mask, masked_blocks=None):
        kv = []
        for j in range(n_blocks):
            start = pl.multiple_of((kb - j) * blk, blk)
            k_all = k_ref[0, pl.ds(start, blk), :].astype(BF16)
            v_all = v_ref[0, pl.ds(start, blk), :].astype(BF16)
            kv.append([(k_all[:, lanes[c // 2]], v_all[:, lanes[c // 2]]) for c in range(n_chains)])
        outs, runs = _sb_sweep(q_heads, kv, tri, [run_ref[c] for c in range(n_chains)],
                               [mask] * n_chains, masked_blocks=masked_blocks)
        for c in range(n_chains):
            run_ref[c] = runs[c]
            acc_ref[c] += outs[c]

    @pl.when(qi == 0)
    def _():
        sweep(qi, 1, causal)

    @pl.when(qi > 0)
    def _():
        sweep(qi, 2, causal, masked_blocks=1)

    def alive():
        return jnp.max(run_ref[...]) > DEAD_LOG

    left = jnp.maximum(qi - 1, 0)
    rest = left % ATTN_UNROLL
    singles = jnp.where(rest > 0, rest, jnp.minimum(left, ATTN_UNROLL))

    def one(carry):
        i, _ = carry
        sweep(qi - 2 - i, 1, None)
        return i + 1, alive()

    _, live = lax.while_loop(lambda c: (c[0] < singles) & c[1], one, (jnp.int32(0), alive()))
    groups = (left - singles) // ATTN_UNROLL

    def many(carry):
        g, _ = carry
        sweep(qi - 2 - singles - g * ATTN_UNROLL, ATTN_UNROLL, None)
        return g + 1, alive()

    lax.while_loop(lambda c: (c[0] < groups) & c[1], many, (jnp.int32(0), live))
    for p, sl in enumerate(lanes):
        o = jnp.where(first, acc_ref[2 * p], acc_ref[2 * p + 1])
        o_ref[0, :, sl] = _head_rms(o, gain_ref[0, :, sl], first).astype(o_ref.dtype)


def _sb_prompt(p, col0, d_sb, gain, tri):
    b, t, _ = p.shape
    group = ATTN_GROUP
    width = group * LANES
    blk = ATTN_BLOCK
    assert t % blk == 0 and d_sb % width == 0 and col0 % width == 0
    n_groups = d_sb // width
    c0 = col0 // width
    return pl.pallas_call(
        functools.partial(_sb_prompt_kernel, blk=blk, group=group),
        grid=(b, n_groups, t // blk),
        in_specs=[
            pl.BlockSpec((1, blk, width), lambda bi, g, qi: (bi, qi, c0 + g)),
            pl.BlockSpec((1, t, width), lambda bi, g, qi: (bi, 0, c0 + n_groups + g)),
            pl.BlockSpec((1, t, width), lambda bi, g, qi: (bi, 0, c0 + 2 * n_groups + g)),
            pl.BlockSpec((blk, blk), lambda bi, g, qi: (0, 0)),
            pl.BlockSpec((1, 1, width), lambda bi, g, qi: (g, 0, 0)),
        ],
        out_specs=pl.BlockSpec((1, blk, width), lambda bi, g, qi: (bi, qi, g)),
        out_shape=jax.ShapeDtypeStruct((b, t, d_sb), BF16),
        scratch_shapes=[pltpu.VMEM((2 * group, blk, LANES), F32), pltpu.VMEM((2 * group, blk, 1), F32)],
        compiler_params=_params("parallel", "parallel", "arbitrary"),
        name="sb_prompt",
    )(p, p, p, tri, gain.reshape(n_groups, 1, width))


def _sb_sample_kernel(q_ref, kn_ref, vn_ref, kc_hbm, vc_hbm, tri_ref, gain_ref, o_ref,
                      kbuf, vbuf, sem, qs_ref, acc_ref, run_ref, *, n_q, n_heads, layer, blk, n_blocks):
    bi = pl.program_id(0)
    rows = n_heads * n_q
    d_sb = n_heads * HEAD_DIM
    past = n_blocks * blk

    def fetch(i, slot):
        start = pl.multiple_of(past - (i + 1) * blk, blk)
        src = lambda hbm: hbm.at[layer, bi, :, pl.ds(start, blk)]
        return (pltpu.make_async_copy(src(kc_hbm), kbuf.at[slot], sem.at[slot, 0]),
                pltpu.make_async_copy(src(vc_hbm), vbuf.at[slot], sem.at[slot, 1]))

    for cp in fetch(0, 0):
        cp.start()

    row_head = lax.broadcasted_iota(jnp.int32, (rows, 1), 0) >> _log2(n_q)
    lane_head = lax.broadcasted_iota(jnp.int32, (1, d_sb), 1) >> _log2(HEAD_DIM)
    q = jnp.tile(q_ref[0] * (HEAD_DIM ** -0.5), (n_heads, 1))
    qs = jnp.where(row_head == lane_head, q, 0.0).astype(BF16)
    qs_ref[...] = qs
    n_new = kn_ref.shape[1]
    q_pos = lax.broadcasted_iota(jnp.int32, (rows, 1), 0) & (n_q - 1)
    mask = lax.broadcasted_iota(jnp.int32, (1, n_new), 1) < q_pos
    outs, runs = _sb_sweep([qs], [(kn_ref[0].astype(BF16), vn_ref[0].astype(BF16))],
                           tri_ref[:n_new, :n_new], [jnp.zeros((rows, 1), F32)], [mask])
    run_ref[...] = runs[0]
    acc_ref[...] = outs[0]

    def alive():
        return jnp.max(run_ref[...]) > DEAD_LOG

    half = rows // 2
    chains = [slice(0, half), slice(half, rows)]
    sub = tri_ref.shape[0]

    def body(carry):
        i, _ = carry
        slot = i & 1
        for cp in fetch(i, slot):
            cp.wait()

        @pl.when(i + 1 < n_blocks)
        def _():
            for cp in fetch(i + 1, 1 - slot):
                cp.start()

        kv = [(kbuf[slot, :, s:s + sub].astype(BF16), vbuf[slot, :, s:s + sub].astype(BF16))
              for s in range(blk - sub, -1, -sub)]
        outs, runs = _sb_sweep([qs_ref[rs, :] for rs in chains], kv, tri_ref[...],
                               [run_ref[rs, :] for rs in chains], [None, None], feature_major=True)
        for rs, o, run in zip(chains, outs, runs):
            run_ref[rs, :] = run
            acc_ref[rs, :] += o
        return i + 1, alive()

    done, _ = lax.while_loop(lambda c: (c[0] < n_blocks) & c[1], body, (jnp.int32(0), alive()))

    @pl.when(done < n_blocks)
    def _():
        for cp in fetch(done, done & 1):
            cp.wait()

    o = jnp.zeros((n_q, d_sb), F32)
    for h in range(n_heads):
        o = o + jnp.where(lane_head == h, acc_ref[h * n_q:(h + 1) * n_q, :], 0.0)
    first = (lax.broadcasted_iota(jnp.int32, (1, LANES), 1) < HEAD_DIM)
    for p in range(d_sb // LANES):
        sl = slice(p * LANES, (p + 1) * LANES)
        o_ref[0, :, sl] = _head_rms(o[:, sl], gain_ref[:, sl], first).astype(o_ref.dtype)


def _sb_sample(p_sb, cache_k, cache_v, layer, gain, tri):
    b, n_q, w = p_sb.shape
    d_sb = w // 3
    n_heads = d_sb // HEAD_DIM
    past = cache_k.shape[3]
    blk = min(CACHE_BLOCK, past)
    assert past % blk == 0 and blk % tri.shape[0] == 0 and n_q <= LANES
    n_blocks = past // blk
    pad = ((0, 0), (0, LANES - n_q), (0, 0))
    k_new = jnp.pad(p_sb[:, :, d_sb:2 * d_sb], pad)
    v_new = jnp.pad(p_sb[:, :, 2 * d_sb:], pad)
    rows = n_heads * n_q
    return pl.pallas_call(
        functools.partial(_sb_sample_kernel, n_q=n_q, n_heads=n_heads, layer=layer, blk=blk,
                          n_blocks=n_blocks),
        grid=(b,),
        in_specs=[
            pl.BlockSpec((1, n_q, d_sb), lambda bi: (bi, 0, 0)),
            pl.BlockSpec((1, LANES, d_sb), lambda bi: (bi, 0, 0)),
            pl.BlockSpec((1, LANES, d_sb), lambda bi: (bi, 0, 0)),
            pl.BlockSpec(memory_space=pl.ANY),
            pl.BlockSpec(memory_space=pl.ANY),
            pl.BlockSpec(tri.shape, lambda bi: (0, 0)),
            pl.BlockSpec((1, d_sb), lambda bi: (0, 0)),
        ],
        out_specs=pl.BlockSpec((1, n_q, d_sb), lambda bi: (bi, 0, 0)),
        out_shape=jax.ShapeDtypeStruct((b, n_q, d_sb), BF16),
        scratch_shapes=[pltpu.VMEM((2, d_sb, blk), cache_k.dtype), pltpu.VMEM((2, d_sb, blk), cache_v.dtype),
                        pltpu.SemaphoreType.DMA((2, 2)),
                        pltpu.VMEM((rows, d_sb), BF16), pltpu.VMEM((rows, d_sb), F32),
                        pltpu.VMEM((rows, 1), F32)],
        compiler_params=_params("arbitrary"),
        name="sb_sample",
    )(p_sb, k_new, v_new, cache_k, cache_v, tri, gain)


def _stack(x, first):
    return jnp.concatenate([jnp.where(first, x, 0.0), jnp.where(first, 0.0, x)], axis=0)


def _each(f, *cols):
    return [f(*xs) for xs in zip(*cols)]


def _sum_sel(x, sel, pieces):
    parts = _split2(x) if pieces == 2 else _split3(x)
    n = x.shape[0]
    out = _dg(jnp.concatenate(parts, axis=0), sel, _NN)
    return sum(out[i * n:(i + 1) * n] for i in range(1, len(parts))) + out[:n]


def _unit_lower_inverses(lows, c):
    n = lows[0].shape[0]
    r = lax.broadcasted_iota(jnp.int32, (n, n), 0)
    col = lax.broadcasted_iota(jnp.int32, (n, n), 1)
    eye = jnp.where(r == col, 1.0, 0.0)
    invs = [eye for _ in lows]
    for lvl in range(_log2(c)):
        same = (r >> (lvl + 1)) == (col >> (lvl + 1))
        lower_left = same & (((r >> lvl) & 1) == 1) & (((col >> lvl) & 1) == 0)
        offs = [jnp.where(lower_left, low, 0.0) for low in lows]
        if lvl == 0:
            invs = [eye + off for off in offs]
            continue
        left = _each(lambda inv, off: _dg(inv, off, _NN), invs, offs)
        invs = _each(lambda inv, x: inv + _dg(x, inv, _NN), invs, left)
    return invs


def _rwkv_mix_kernel(p_ref, shift_ref, mu_ref, w0_ref, w2_ref, a0_ref, a2_ref, g2_ref,
                     kk_ref, ka_ref, rk_ref, lnw_ref, lnb_ref, s0_ref, o_ref, sT_ref, s_ref, prev_ref,
                     *, c, t_valid, d_rw, wd, wa):
    ci = pl.program_id(1)

    @pl.when(ci == 0)
    def _():
        s_ref[...] = s0_ref[0]
        prev_ref[...] = shift_ref[0]

    @pl.when(ci * c < t_valid)
    def _():
        p = p_ref[0]
        row = lax.broadcasted_iota(jnp.int32, (c, 1), 0)
        prev = jnp.where(row == 0, prev_ref[...], pltpu.roll(p, 1, 0))
        prev_ref[...] = p[c - 1:c, :]
        xs = p + (prev - p) * mu_ref[...]
        o3 = 3 * d_rw
        dw = xs[:, o3:o3 + wd]
        da = xs[:, o3 + wd:o3 + wd + wa]
        dg = xs[:, o3 + wd + wa:]
        w_pre = w0_ref[...] + _dg(jnp.tanh(dw), w2_ref[...], _NN)
        w_log = -(jnp.maximum(-w_pre, 0.0) + jnp.log1p(jnp.exp(-jnp.abs(w_pre)))) - 0.5
        streams = (xs[:, :d_rw], xs[:, d_rw:2 * d_rw], xs[:, 2 * d_rw:o3],
                   -jnp.exp(w_log),
                   jax.nn.sigmoid(a0_ref[...] + _dg(da, a2_ref[...], _NN)),
                   _dg(jax.nn.sigmoid(dg), g2_ref[...], _NN))
        _rwkv_scan_chunk(ci, streams, kk_ref, ka_ref, rk_ref, lnw_ref, lnb_ref, o_ref, s_ref,
                         c=c, t_valid=t_valid, group=d_rw // LANES)

    @pl.when(ci * c >= t_valid)
    def _():
        o_ref[...] = jnp.zeros_like(o_ref)

    @pl.when(ci == pl.num_programs(1) - 1)
    def _():
        sT_ref[0] = s_ref[...]


def _rwkv_scan_chunk(ci, streams, kk_ref, ka_ref, rk_ref, lnw_ref, lnb_ref, o_ref, s_ref, *,
                     c, t_valid, group):
    lanes = [slice(p * LANES, (p + 1) * LANES) for p in range(group)]
    take = lambda x: [x[:, sl] for sl in lanes]
    par = lambda ref: [ref[:, sl] for sl in lanes]
    r_all, k_all, v_all, lw_all, al_all, g_all = streams
    n = 2 * c
    first = lax.broadcasted_iota(jnp.int32, (1, LANES), 1) < HEAD_DIM
    valid = (ci * c + lax.broadcasted_iota(jnp.int32, (c, 1), 0)) < t_valid
    pair_sum = _head_pair_sum_matrix()
    pair_mean = _head_pair_sum_matrix(1.0 / HEAD_DIM)
    t_i = lax.broadcasted_iota(jnp.int32, (c, c), 0)
    s_i = lax.broadcasted_iota(jnp.int32, (c, c), 1)
    upto = jnp.where(s_i <= t_i, 1.0, 0.0).astype(BF16)
    tt = lax.broadcasted_iota(jnp.int32, (n, n), 0) & (c - 1)
    ss = lax.broadcasted_iota(jnp.int32, (n, n), 1) & (c - 1)
    strict, incl = ss < tt, ss <= tt
    diag = (lax.broadcasted_iota(jnp.int32, (LANES, LANES), 0)
            == lax.broadcasted_iota(jnp.int32, (LANES, LANES), 1))

    r = take(r_all)
    k = [jnp.where(valid, x, 0.0) for x in take(k_all)]
    v = [jnp.where(valid, x, 0.0) for x in take(v_all)]
    lw = [jnp.where(valid, x, 0.0) for x in take(lw_all)]
    al = take(al_all)
    g = take(g_all)

    kk = _each(lambda x, w: x * w, k, par(kk_ref))
    norm = [jnp.sqrt(_sum_sel(x * x, pair_sum, 2)) for x in kk]
    kk = _each(lambda x, nr: x / jnp.maximum(nr, KK_EPS), kk, norm)
    k_mod = _each(lambda x, a_, w: x * (1.0 + (a_ - 1.0) * w), k, al, par(ka_ref))
    b = _each(lambda x, a_: x * a_, kk, al)

    def cumsum(x):
        out = _dg(upto, jnp.concatenate(_split3(x), axis=1), _NN)
        return out[:, :LANES] + out[:, LANES:2 * LANES] + out[:, 2 * LANES:]

    cum = [cumsum(x) for x in lw]
    cum_end = [x[c - 1:c, :] for x in cum]
    dec_out = [jnp.exp(-x) for x in cum]
    dec_rest = _each(lambda e, x: jnp.exp(e - x), cum_end, cum)
    a_t = _each(lambda x, cu, l: _stack(-x * jnp.exp(cu - l), first), kk, cum, lw)
    r_t = _each(lambda x, cu: _stack(x * jnp.exp(cu), first), r, cum)
    b_t = _each(lambda x, d: _stack(x * d, first), b, dec_out)
    k_t = _each(lambda x, d: _stack(x * d, first), k_mod, dec_out)
    b_h = _each(lambda x, d: _stack(x * d, first), b, dec_rest)
    k_h = _each(lambda x, d: _stack(x * d, first), k_mod, dec_rest)
    v_s = [_stack(x, first) for x in v]

    prod = _each(lambda a_, r_, b_, k_: _dg(jnp.concatenate([a_, r_], axis=0),
                                            jnp.concatenate([b_, k_], axis=0), _NT), a_t, r_t, b_t, k_t)
    l_ab = [jnp.where(strict, x[:n, :n], 0.0) for x in prod]
    l_ak = [jnp.where(strict, x[:n, n:], 0.0) for x in prod]
    m_r = [jnp.where(jnp.concatenate([incl, incl], axis=1), x[n:], 0.0) for x in prod]

    solve = _unit_lower_inverses(l_ab, c)
    lv = _each(lambda l_, v_: _dg(l_, v_, _NN), l_ak, v_s)
    au = _each(lambda t_, a_, x: _dg(t_, jnp.concatenate([a_, x], axis=1), _NN), solve, a_t, lv)
    rhs = _each(lambda x, v_: jnp.concatenate(
        [x, jnp.concatenate([jnp.zeros_like(v_), v_], axis=1)], axis=0), au, v_s)
    ry = _each(lambda m, x: _dg(m, x, _NN), m_r, rhs)
    gh = _each(lambda b_, k_, x: _dg(jnp.concatenate([b_, k_], axis=0), x, _TN), b_h, k_h, rhs)

    lhs = _each(lambda r_, x, y, e: jnp.concatenate(
        [r_ + x[:, :LANES], jnp.where(diag, jnp.exp(e), 0.0) + y[:, :LANES]], axis=0), r_t, ry, gh, cum_end)
    step = [_dg3(x, s_ref[p]) for p, x in enumerate(lhs)]
    for p in range(group):
        s_ref[p] = step[p][n:] + gh[p][:, LANES:]
    y = _each(lambda st, x: st[:c] + st[c:n] + x[:c, LANES:] + x[c:, LANES:], step, ry)

    mean = [_sum_sel(x, pair_mean, 2) for x in y]
    d = _each(lambda x, m: x - m, y, mean)
    var = [_sum_sel(x * x, pair_mean, 2) for x in d]
    bonus = _each(lambda r_, k_, w: _sum_sel(r_ * k_ * w, pair_sum, 2), r, k_mod, par(rk_ref))
    for p, sl in enumerate(lanes):
        yn = d[p] * lax.rsqrt(var[p] + GN_EPS) * lnw_ref[:, sl] + lnb_ref[:, sl]
        o_ref[0, :, sl] = ((yn + bonus[p] * v[p]) * g[p]).astype(o_ref.dtype)


def _rwkv_mix(p, pw, shift0, mu, w0, w2, a0, a2, g2, k_k, k_a, r_k, ln_w, ln_b, s0, t_valid):
    b, t_in, _ = p.shape
    d_rw = w0.shape[1]
    pairs = d_rw // LANES
    c = SCAN_CHUNK
    t = -(-t_in // c) * c
    if t != t_in:
        p = jnp.pad(p, ((0, 0), (0, t - t_in), (0, 0)))
    full = lambda a: pl.BlockSpec(a.shape, lambda bi, ci: (0,) * a.ndim)
    st_spec = pl.BlockSpec((1, pairs, LANES, LANES), lambda bi, ci: (bi, 0, 0, 0))
    params = (mu, w0, w2, a0, a2, g2, k_k, k_a, r_k, ln_w, ln_b)
    o, s_t = pl.pallas_call(
        functools.partial(_rwkv_mix_kernel, c=c, t_valid=t_valid, d_rw=d_rw, wd=w2.shape[0],
                          wa=a2.shape[0]),
        grid=(b, t // c),
        in_specs=[pl.BlockSpec((1, c, pw), lambda bi, ci: (bi, ci, 0)),
                  pl.BlockSpec((1, 1, pw), lambda bi, ci: (bi, 0, 0))]
                 + [full(a) for a in params] + [st_spec],
        out_specs=[pl.BlockSpec((1, c, d_rw), lambda bi, ci: (bi, ci, 0)), st_spec],
        out_shape=[jax.ShapeDtypeStruct((b, t, d_rw), BF16),
                   jax.ShapeDtypeStruct((b, pairs, LANES, LANES), F32)],
        scratch_shapes=[pltpu.VMEM((pairs, LANES, LANES), F32), pltpu.VMEM((1, pw), F32)],
        compiler_params=_params("parallel", "arbitrary"),
        name="rwkv_mix",
    )(p, shift0, *params, s0)
    return o[:, :t_in], s_t


def _state_to_pairs(s):
    b, h, e, _ = s.shape
    st = jnp.swapaxes(s, -1, -2).reshape(b, h // 2, 2, e, e)
    z = jnp.zeros_like(st[:, :, 0])
    top = jnp.concatenate([st[:, :, 0], z], axis=-1)
    bot = jnp.concatenate([z, st[:, :, 1]], axis=-1)
    return jnp.concatenate([top, bot], axis=-2)


def _pairs_to_state(sp):
    b, pairs = sp.shape[:2]
    e = HEAD_DIM
    blocks = jnp.stack([sp[:, :, :e, :e], sp[:, :, e:, e:]], axis=2)
    return jnp.swapaxes(blocks, -1, -2).reshape(b, 2 * pairs, e, e)


def _pad_cols(x, width):
    return jnp.pad(x, [(0, 0)] * (x.ndim - 1) + [(0, width - x.shape[-1])])


def _rw_layout(x, d_rw, ranks, widths):
    parts = [x[..., :3 * d_rw]]
    o = 3 * d_rw
    for rank, width in zip(ranks, widths):
        parts.append(_pad_cols(x[..., o:o + rank], width))
        o += rank
    return jnp.concatenate(parts, axis=-1)


def _rw_unlayout(x, d_rw, ranks, widths):
    parts = [x[..., :3 * d_rw]]
    o = 3 * d_rw
    for rank, width in zip(ranks, widths):
        parts.append(x[..., o:o + rank])
        o += width
    return jnp.concatenate(parts, axis=-1)


def _layer(h, lw, cache, s0_pairs, shift0, t_valid, tri):
    b, t, d = h.shape
    d_sb = lw["sb_gain"].shape[-1]
    x = h.reshape(b * t, d)
    x = _ffn(x, lw["ffn1_pre"], lw["ffn1_post"], lw["ffn1_gate"], lw["ffn1_up"], lw["ffn1_down"])
    p = _norm_matmul(x, lw["mix_pre"], lw["w_in"], MIX_IN_COLS).reshape(b, t, -1)
    rw_cols = p.shape[-1] - 3 * d_sb
    if cache is None:
        o_sb = _sb_prompt(p, rw_cols, d_sb, lw["sb_gain"], tri)
    else:
        o_sb = _sb_sample(p[:, :, rw_cols:], cache[0], cache[1], cache[2], lw["sb_gain"], tri)
    o_rw, s_t = _rwkv_mix(p, rw_cols, shift0, lw["mu"], lw["w0"], lw["w2"], lw["a0"], lw["a2"], lw["g2"],
                          lw["k_k"], lw["k_a"], lw["r_k"], lw["ln_w"], lw["ln_b"], s0_pairs, t_valid)
    x = _mix_out(o_sb.reshape(b * t, -1), o_rw.reshape(b * t, -1), lw["w_out_sb"], lw["w_out_rw"],
                 x, lw["mix_post"])
    x = _ffn(x, lw["ffn2_pre"], lw["ffn2_post"], lw["ffn2_gate"], lw["ffn2_up"], lw["ffn2_down"])
    k_new = p[:, :t_valid, rw_cols + d_sb:rw_cols + 2 * d_sb]
    v_new = p[:, :t_valid, rw_cols + 2 * d_sb:]
    return x.reshape(b, t, d), k_new, v_new, s_t, p[:, t_valid - 1, :rw_cols]


def kernel(x_prompt, x_sample, cache_sb_k, cache_sb_v, state_rwkv_S, state_rwkv_shift, meta_tokens, ffn1_norm_pre, ffn1_norm_post, ffn1_w_gate, ffn1_w_up, ffn1_w_down, mix_norm_pre, mix_norm_post, w_in, sb_out_gain, rwkv_mu, rwkv_w0, rwkv_w2, rwkv_a0, rwkv_a2, rwkv_g2, rwkv_k_k, rwkv_k_a, rwkv_r_k, rwkv_ln_w, rwkv_ln_b, w_out, ffn2_norm_pre, ffn2_norm_post, ffn2_w_gate, ffn2_w_up, ffn2_w_down):
    depth, d = ffn1_norm_pre.shape
    bp, seq, _ = x_prompt.shape
    bs, dec_seq, _ = x_sample.shape
    n_meta = meta_tokens.shape[0]
    h_sb = sb_out_gain.shape[1]
    d_sb = h_sb * HEAD_DIM
    d_rw = rwkv_w0.shape[1]
    h_rw = d_rw // HEAD_DIM
    ranks = (rwkv_w2.shape[1], rwkv_a2.shape[1], rwkv_g2.shape[1])
    widths = tuple(-(-r // LANES) * LANES for r in ranks)
    p_rw_cols = 3 * d_rw + sum(ranks)
    past = cache_sb_k.shape[2]

    t_valid = n_meta + seq
    t_pad = -(-t_valid // ATTN_BLOCK) * ATTN_BLOCK
    meta = jnp.broadcast_to(meta_tokens[None].astype(x_prompt.dtype), (bp, n_meta, d))
    hp = jnp.concatenate([meta, x_prompt, jnp.zeros((bp, t_pad - t_valid, d), x_prompt.dtype)], axis=1)
    hs = x_sample

    tri = _strict_upper(ATTN_BLOCK)
    feature_major = lambda c: jnp.transpose(c, (0, 1, 3, 4, 2)).reshape(depth, bs, d_sb, past)
    cache_kt, cache_vt = feature_major(cache_sb_k), feature_major(cache_sb_v)
    row = lambda x: x.reshape(1, -1)
    pad_rows = lambda w, width: jnp.pad(w, ((0, width - w.shape[0]), (0, 0)))
    w_in_rw = _rw_layout(w_in[:, :, 3 * d_sb:], d_rw, ranks, widths)
    w_in_all = jnp.concatenate([w_in_rw, w_in[:, :, :3 * d_sb]], axis=-1).astype(BF16)
    ffn_w = [_to_bf16(w) for w in (ffn1_w_gate, ffn1_w_up, ffn1_w_down, ffn2_w_gate, ffn2_w_up, ffn2_w_down)]
    mu_all = _rw_layout(rwkv_mu, d_rw, ranks, widths)

    outs = [[] for _ in range(8)]
    s0_p = jnp.zeros((bp, h_rw // 2, LANES, LANES), F32)
    shift0_p = jnp.zeros((bp, 1, w_in_rw.shape[-1]), F32)
    for l in range(depth):
        lw = dict(
            ffn1_pre=row(ffn1_norm_pre[l]), ffn1_post=row(ffn1_norm_post[l]),
            ffn1_gate=ffn_w[0][l], ffn1_up=ffn_w[1][l], ffn1_down=ffn_w[2][l],
            ffn2_pre=row(ffn2_norm_pre[l]), ffn2_post=row(ffn2_norm_post[l]),
            ffn2_gate=ffn_w[3][l], ffn2_up=ffn_w[4][l], ffn2_down=ffn_w[5][l],
            mix_pre=row(mix_norm_pre[l]), mix_post=row(mix_norm_post[l]),
            w_in=w_in_all[l], sb_gain=row(sb_out_gain[l]),
            mu=row(mu_all[l]), w0=row(rwkv_w0[l]), a0=row(rwkv_a0[l]),
            w2=pad_rows(rwkv_w2[l], widths[0]).astype(BF16),
            a2=pad_rows(rwkv_a2[l], widths[1]).astype(BF16),
            g2=pad_rows(rwkv_g2[l], widths[2]).astype(BF16),
            k_k=row(rwkv_k_k[l]), k_a=row(rwkv_k_a[l]), r_k=row(rwkv_r_k[l]),
            ln_w=row(rwkv_ln_w[l]), ln_b=row(rwkv_ln_b[l]),
            w_out_sb=w_out[l, :d_sb].astype(BF16), w_out_rw=w_out[l, d_sb:].astype(BF16),
        )
        hp, kp, vp, sp, shp = _layer(hp, lw, None, s0_p, shift0_p, t_valid, tri)
        cache = (cache_kt, cache_vt, l)
        shift0_s = _rw_layout(state_rwkv_shift[l], d_rw, ranks, widths)[:, None, :]
        hs, ks, vs, ss, shs = _layer(hs, lw, cache, _state_to_pairs(state_rwkv_S[l].astype(F32)),
                                     shift0_s, dec_seq, tri)
        per_layer = (
            kp.reshape(bp, t_valid, h_sb, HEAD_DIM), vp.reshape(bp, t_valid, h_sb, HEAD_DIM),
            _pairs_to_state(sp).astype(state_rwkv_S.dtype),
            _rw_unlayout(shp, d_rw, ranks, widths).astype(state_rwkv_shift.dtype),
            ks.reshape(bs, dec_seq, h_sb, HEAD_DIM), vs.reshape(bs, dec_seq, h_sb, HEAD_DIM),
            _pairs_to_state(ss).astype(state_rwkv_S.dtype),
            _rw_unlayout(shs, d_rw, ranks, widths).astype(state_rwkv_shift.dtype),
        )
        for acc, val in zip(outs, per_layer):
            acc.append(val)
    assert p_rw_cols == state_rwkv_shift.shape[-1]
    y_prompt = hp[:, n_meta:t_valid]
    return (y_prompt, hs) + tuple(jnp.stack(o) for o in outs)
```

```python
import functools

import jax
import jax.numpy as jnp
from jax import lax
from jax.experimental import pallas as pl
from jax.experimental.pallas import tpu as pltpu

F32 = jnp.float32
BF16 = jnp.bfloat16

HEAD_DIM = 64
LANES = 128
NORM_EPS = 1e-6
GN_EPS = 64e-5
KK_EPS = 1e-12
VMEM_LIMIT_BYTES = 56 * 2 ** 20

ATTN_BLOCK = 256
ATTN_GROUP = 2
ATTN_UNROLL = 2
CACHE_BLOCK = 512
DEAD_LOG = -104.0
MIX_IN_COLS = 1664
SCAN_CHUNK = 64
ROW_TILE = 512
FF_TILE = 512
CAST_BLOCK_BYTES = 8 * 2 ** 20


def _params(*sem):
    return pltpu.CompilerParams(dimension_semantics=sem, vmem_limit_bytes=VMEM_LIMIT_BYTES)


def _tile(n, pref, mult=16):
    t = min(pref, n)
    t -= t % mult
    while t > mult and n % t:
        t -= mult
    assert t >= mult and n % t == 0, (n, pref)
    return t


def _log2(n):
    k = n.bit_length() - 1
    assert 1 << k == n, n
    return k


_NN = (((1,), (0,)), ((), ()))
_NT = (((1,), (1,)), ((), ()))
_TN = (((0,), (0,)), ((), ()))


def _dg(a, b, dims):
    return lax.dot_general(a.astype(BF16), b.astype(BF16), dims, preferred_element_type=F32)


def _split2(x):
    hi = x.astype(BF16)
    lo = (x - hi.astype(F32)).astype(BF16)
    return hi, lo


def _split3(x):
    hi = x.astype(BF16)
    r = x - hi.astype(F32)
    mid = r.astype(BF16)
    lo = (r - mid.astype(F32)).astype(BF16)
    return hi, mid, lo


def _dg3(a, b, dims=_NN):
    ah, al = _split2(a)
    bh, bl = _split2(b)
    return _dg(ah, bh, dims) + (_dg(ah, bl, dims) + _dg(al, bh, dims))


def _rms(x, eps=NORM_EPS):
    return x * lax.rsqrt(jnp.mean(x * x, axis=-1, keepdims=True) + eps)


def _head_pair_sum_matrix(scale=1.0):
    r = lax.broadcasted_iota(jnp.int32, (LANES, LANES), 0) >> _log2(HEAD_DIM)
    c = lax.broadcasted_iota(jnp.int32, (LANES, LANES), 1) >> _log2(HEAD_DIM)
    return jnp.where(r == c, scale, 0.0).astype(BF16)


def _cast_kernel(x_ref, o_ref):
    o_ref[...] = x_ref[...].astype(o_ref.dtype)


def _to_bf16(x):
    flat = x.reshape(-1, x.shape[-1])
    n, w = flat.shape
    tm = _tile(n, max(16, CAST_BLOCK_BYTES // (4 * w)))
    out = pl.pallas_call(
        _cast_kernel,
        grid=(n // tm,),
        in_specs=[pl.BlockSpec((tm, w), lambda i: (i, 0))],
        out_specs=pl.BlockSpec((tm, w), lambda i: (i, 0)),
        out_shape=jax.ShapeDtypeStruct((n, w), BF16),
        compiler_params=_params("parallel"),
        name="to_bf16",
    )(flat)
    return out.reshape(x.shape)


def _ffn_kernel(x_ref, gpre_ref, gpost_ref, wg_ref, wu_ref, wd_ref, o_ref, u_ref, acc_ref):
    f = pl.program_id(1)

    @pl.when(f == 0)
    def _():
        u_ref[...] = (_rms(x_ref[...]) * gpre_ref[...]).astype(BF16)
        acc_ref[...] = jnp.zeros_like(acc_ref)

    u = u_ref[...]
    gate = jnp.dot(u, wg_ref[...], preferred_element_type=F32)
    up = jnp.dot(u, wu_ref[...], preferred_element_type=F32)
    act = (gate * jax.nn.sigmoid(gate) * up).astype(BF16)
    acc_ref[...] += jnp.dot(act, wd_ref[...], preferred_element_type=F32)

    @pl.when(f == pl.num_programs(1) - 1)
    def _():
        o_ref[...] = x_ref[...] + 0.5 * (_rms(acc_ref[...]) * gpost_ref[...])


def _ffn(x, g_pre, g_post, w_gate, w_up, w_down):
    n, d = x.shape
    d_ff = w_gate.shape[1]
    tm = _tile(n, ROW_TILE)
    tf = _tile(d_ff, FF_TILE, LANES)
    return pl.pallas_call(
        _ffn_kernel,
        grid=(n // tm, d_ff // tf),
        in_specs=[
            pl.BlockSpec((tm, d), lambda i, f: (i, 0)),
            pl.BlockSpec((1, d), lambda i, f: (0, 0)),
            pl.BlockSpec((1, d), lambda i, f: (0, 0)),
            pl.BlockSpec((d, tf), lambda i, f: (0, f)),
            pl.BlockSpec((d, tf), lambda i, f: (0, f)),
            pl.BlockSpec((tf, d), lambda i, f: (f, 0)),
        ],
        out_specs=pl.BlockSpec((tm, d), lambda i, f: (i, 0)),
        out_shape=jax.ShapeDtypeStruct((n, d), F32),
        scratch_shapes=[pltpu.VMEM((tm, d), BF16), pltpu.VMEM((tm, d), F32)],
        compiler_params=_params("parallel", "arbitrary"),
        name="ffn_half",
    )(x, g_pre, g_post, w_gate, w_up, w_down)


def _norm_matmul_kernel(x_ref, g_ref, w_ref, o_ref):
    u = (_rms(x_ref[...]) * g_ref[...]).astype(BF16)
    o_ref[...] = jnp.dot(u, w_ref[...], preferred_element_type=F32)


def _norm_matmul(x, g, w, col_tile):
    n, d = x.shape
    cols = w.shape[1]
    tm = _tile(n, ROW_TILE)
    tn = _tile(cols, col_tile, LANES)
    return pl.pallas_call(
        _norm_matmul_kernel,
        grid=(cols // tn, n // tm),
        in_specs=[
            pl.BlockSpec((tm, d), lambda j, i: (i, 0)),
            pl.BlockSpec((1, d), lambda j, i: (0, 0)),
            pl.BlockSpec((d, tn), lambda j, i: (0, j)),
        ],
        out_specs=pl.BlockSpec((tm, tn), lambda j, i: (i, j)),
        out_shape=jax.ShapeDtypeStruct((n, cols), F32),
        compiler_params=_params("parallel", "parallel"),
        name="mix_in",
    )(x, g, w)


def _mix_out_kernel(osb_ref, orw_ref, wsb_ref, wrw_ref, h_ref, g_ref, o_ref):
    o = jnp.dot(osb_ref[...], wsb_ref[...], preferred_element_type=F32)
    o = o + jnp.dot(orw_ref[...], wrw_ref[...], preferred_element_type=F32)
    o_ref[...] = h_ref[...] + _rms(o) * g_ref[...]


def _mix_out(o_sb, o_rw, w_sb, w_rw, h, g):
    n, d = h.shape
    tm = _tile(n, ROW_TILE // 2)
    return pl.pallas_call(
        _mix_out_kernel,
        grid=(n // tm,),
        in_specs=[
            pl.BlockSpec((tm, o_sb.shape[1]), lambda i: (i, 0)),
            pl.BlockSpec((tm, o_rw.shape[1]), lambda i: (i, 0)),
            pl.BlockSpec(w_sb.shape, lambda i: (0, 0)),
            pl.BlockSpec(w_rw.shape, lambda i: (0, 0)),
            pl.BlockSpec((tm, d), lambda i: (i, 0)),
            pl.BlockSpec((1, d), lambda i: (0, 0)),
        ],
        out_specs=pl.BlockSpec((tm, d), lambda i: (i, 0)),
        out_shape=jax.ShapeDtypeStruct((n, d), F32),
        compiler_params=_params("parallel"),
        name="mix_out",
    )(o_sb, o_rw, w_sb, w_rw, h, g)


def _strict_upper(n):
    r = lax.broadcasted_iota(jnp.int32, (n, n), 0)
    c = lax.broadcasted_iota(jnp.int32, (n, n), 1)
    return jnp.where(r > c, 1.0, 0.0).astype(BF16)


def _split_top16(x):
    bits = lax.bitcast_convert_type(x, jnp.uint32) & jnp.uint32(0xFFFF0000)
    hi = lax.bitcast_convert_type(bits, F32)
    return hi.astype(BF16), (x - hi).astype(BF16)


def _sb_sweep(qs, kv_blocks, tri, runs, masks, feature_major=False):
    jobs = []
    for block in kv_blocks:
        for c, (q, mask) in enumerate(zip(qs, masks)):
            k_blk, v_blk = block[c] if isinstance(block, list) else block
            z = _dg(q, k_blk, _NN if feature_major else _NT)
            nz = -z
            soft = jnp.log(1.0 + jnp.exp(jnp.minimum(z, nz)))
            stay = jnp.minimum(nz, 0.0) - soft
            log_sig = stay + z
            if mask is not None:
                stay = jnp.where(mask, stay, 0.0)
            stacked = jnp.concatenate(_split_top16(stay), axis=0)
            jobs.append((c, v_blk, log_sig, stacked, stay[:, :1], mask))
    between = [jnp.dot(job[3], tri, preferred_element_type=F32) for job in jobs]
    runs = list(runs)
    outs = [None] * len(qs)
    for (c, v_blk, log_sig, _, stay0, mask), btw in zip(jobs, between):
        rows = log_sig.shape[0]
        right = btw[:rows] + btw[rows:]
        a = jnp.exp(log_sig + right)
        if mask is not None:
            a = jnp.where(mask, a, 0.0)
        o = jnp.exp(runs[c]) * _dg(a, v_blk, _NT if feature_major else _NN)
        outs[c] = o if outs[c] is None else outs[c] + o
        runs[c] = runs[c] + (right[:, :1] + stay0)
    return outs, runs


def _head_rms(o, gain, first):
    sq = o * o
    s0 = jnp.sum(jnp.where(first, sq, 0.0), axis=-1, keepdims=True)
    s1 = jnp.sum(jnp.where(first, 0.0, sq), axis=-1, keepdims=True)
    ms = jnp.where(first, s0, s1) * (1.0 / HEAD_DIM)
    return o * lax.rsqrt(ms + NORM_EPS) * gain


def _sb_prompt_kernel(q_ref, k_ref, v_ref, tri_ref, gain_ref, o_ref, acc_ref, run_ref, *, blk, group):
    qi = pl.program_id(2)
    first = lax.broadcasted_iota(jnp.int32, (1, LANES), 1) < HEAD_DIM
    lanes = [slice(p * LANES, (p + 1) * LANES) for p in range(group)]
    q_heads = []
    for sl in lanes:
        q = q_ref[0, :, sl] * (HEAD_DIM ** -0.5)
        q_heads += [jnp.where(first, q, 0.0).astype(BF16), jnp.where(first, 0.0, q).astype(BF16)]
    n_chains = len(q_heads)
    tri = tri_ref[...]
    causal = (lax.broadcasted_iota(jnp.int32, (blk, blk), 1)
              < lax.broadcasted_iota(jnp.int32, (blk, blk), 0))

    acc_ref[...] = jnp.zeros_like(acc_ref)
    run_ref[...] = jnp.zeros_like(run_ref)

    def sweep(kb, n_blocks, mask):
        kv = []
        for j in range(n_blocks):
            start = pl.multiple_of((kb - j) * blk, blk)
            k_all = k_ref[0, pl.ds(start, blk), :].astype(BF16)
            v_all = v_ref[0, pl.ds(start, blk), :].astype(BF16)
            kv.append([(k_all[:, lanes[c // 2]], v_all[:, lanes[c // 2]]) for c in range(n_chains)])
        outs, runs = _sb_sweep(q_heads, kv, tri, [run_ref[c] for c in range(n_chains)],
                               [mask] * n_chains)
        for c in range(n_chains):
            run_ref[c] = runs[c]
            acc_ref[c] += outs[c]

    sweep(qi, 1, causal)

    def alive():
        return jnp.max(run_ref[...]) > DEAD_LOG

    rest = qi % ATTN_UNROLL
    singles = jnp.where(rest > 0, rest, jnp.minimum(qi, ATTN_UNROLL))

    def one(carry):
        i, _ = carry
        sweep(qi - 1 - i, 1, None)
        return i + 1, alive()

    _, live = lax.while_loop(lambda c: (c[0] < singles) & c[1], one, (jnp.int32(0), qi >= 0))
    groups = (qi - singles) // ATTN_UNROLL

    def many(carry):
        g, _ = carry
        sweep(qi - 1 - singles - g * ATTN_UNROLL, ATTN_UNROLL, None)
        return g + 1, alive()

    lax.while_loop(lambda c: (c[0] < groups) & c[1], many, (jnp.int32(0), live))
    for p, sl in enumerate(lanes):
        o = jnp.where(first, acc_ref[2 * p], acc_ref[2 * p + 1])
        o_ref[0, :, sl] = _head_rms(o, gain_ref[0, :, sl], first).astype(o_ref.dtype)


def _sb_prompt(p, col0, d_sb, gain, tri):
    b, t, _ = p.shape
    group = ATTN_GROUP
    width = group * LANES
    blk = ATTN_BLOCK
    assert t % blk == 0 and d_sb % width == 0 and col0 % width == 0
    n_groups = d_sb // width
    c0 = col0 // width
    return pl.pallas_call(
        functools.partial(_sb_prompt_kernel, blk=blk, group=group),
        grid=(b, n_groups, t // blk),
        in_specs=[
            pl.BlockSpec((1, blk, width), lambda bi, g, qi: (bi, qi, c0 + g)),
            pl.BlockSpec((1, t, width), lambda bi, g, qi: (bi, 0, c0 + n_groups + g)),
            pl.BlockSpec((1, t, width), lambda bi, g, qi: (bi, 0, c0 + 2 * n_groups + g)),
            pl.BlockSpec((blk, blk), lambda bi, g, qi: (0, 0)),
            pl.BlockSpec((1, 1, width), lambda bi, g, qi: (g, 0, 0)),
        ],
        out_specs=pl.BlockSpec((1, blk, width), lambda bi, g, qi: (bi, qi, g)),
        out_shape=jax.ShapeDtypeStruct((b, t, d_sb), BF16),
        scratch_shapes=[pltpu.VMEM((2 * group, blk, LANES), F32), pltpu.VMEM((2 * group, blk, 1), F32)],
        compiler_params=_params("parallel", "parallel", "arbitrary"),
        name="sb_prompt",
    )(p, p, p, tri, gain.reshape(n_groups, 1, width))


def _sb_sample_kernel(q_ref, kn_ref, vn_ref, kc_hbm, vc_hbm, tri_ref, gain_ref, o_ref,
                      kbuf, vbuf, sem, qs_ref, acc_ref, run_ref, *, n_q, n_heads, layer, blk, n_blocks):
    bi = pl.program_id(0)
    rows = n_heads * n_q
    d_sb = n_heads * HEAD_DIM
    past = n_blocks * blk

    def fetch(i, slot):
        start = pl.multiple_of(past - (i + 1) * blk, blk)
        src = lambda hbm: hbm.at[layer, bi, :, pl.ds(start, blk)]
        return (pltpu.make_async_copy(src(kc_hbm), kbuf.at[slot], sem.at[slot, 0]),
                pltpu.make_async_copy(src(vc_hbm), vbuf.at[slot], sem.at[slot, 1]))

    for cp in fetch(0, 0):
        cp.start()

    row_head = lax.broadcasted_iota(jnp.int32, (rows, 1), 0) >> _log2(n_q)
    lane_head = lax.broadcasted_iota(jnp.int32, (1, d_sb), 1) >> _log2(HEAD_DIM)
    q = jnp.tile(q_ref[0] * (HEAD_DIM ** -0.5), (n_heads, 1))
    qs = jnp.where(row_head == lane_head, q, 0.0).astype(BF16)
    qs_ref[...] = qs
    n_new = kn_ref.shape[1]
    q_pos = lax.broadcasted_iota(jnp.int32, (rows, 1), 0) & (n_q - 1)
    mask = lax.broadcasted_iota(jnp.int32, (1, n_new), 1) < q_pos
    outs, runs = _sb_sweep([qs], [(kn_ref[0].astype(BF16), vn_ref[0].astype(BF16))],
                           tri_ref[:n_new, :n_new], [jnp.zeros((rows, 1), F32)], [mask])
    run_ref[...] = runs[0]
    acc_ref[...] = outs[0]

    def alive():
        return jnp.max(run_ref[...]) > DEAD_LOG

    half = rows // 2
    chains = [slice(0, half), slice(half, rows)]
    sub = tri_ref.shape[0]

    def body(carry):
        i, _ = carry
        slot = i & 1
        for cp in fetch(i, slot):
            cp.wait()

        @pl.when(i + 1 < n_blocks)
        def _():
            for cp in fetch(i + 1, 1 - slot):
                cp.start()

        kv = [(kbuf[slot, :, s:s + sub].astype(BF16), vbuf[slot, :, s:s + sub].astype(BF16))
              for s in range(blk - sub, -1, -sub)]
        outs, runs = _sb_sweep([qs_ref[rs, :] for rs in chains], kv, tri_ref[...],
                               [run_ref[rs, :] for rs in chains], [None, None], feature_major=True)
        for rs, o, run in zip(chains, outs, runs):
            run_ref[rs, :] = run
            acc_ref[rs, :] += o
        return i + 1, alive()

    done, _ = lax.while_loop(lambda c: (c[0] < n_blocks) & c[1], body, (jnp.int32(0), alive()))

    @pl.when(done < n_blocks)
    def _():
        for cp in fetch(done, done & 1):
            cp.wait()

    o = jnp.zeros((n_q, d_sb), F32)
    for h in range(n_heads):
        o = o + jnp.where(lane_head == h, acc_ref[h * n_q:(h + 1) * n_q, :], 0.0)
    first = (lax.broadcasted_iota(jnp.int32, (1, LANES), 1) < HEAD_DIM)
    for p in range(d_sb // LANES):
        sl = slice(p * LANES, (p + 1) * LANES)
        o_ref[0, :, sl] = _head_rms(o[:, sl], gain_ref[:, sl], first).astype(o_ref.dtype)


def _sb_sample(p_sb, cache_k, cache_v, layer, gain, tri):
    b, n_q, w = p_sb.shape
    d_sb = w // 3
    n_heads = d_sb // HEAD_DIM
    past = cache_k.shape[3]
    blk = min(CACHE_BLOCK, past)
    assert past % blk == 0 and blk % tri.shape[0] == 0 and n_q <= LANES
    n_blocks = past // blk
    pad = ((0, 0), (0, LANES - n_q), (0, 0))
    k_new = jnp.pad(p_sb[:, :, d_sb:2 * d_sb], pad)
    v_new = jnp.pad(p_sb[:, :, 2 * d_sb:], pad)
    rows = n_heads * n_q
    return pl.pallas_call(
        functools.partial(_sb_sample_kernel, n_q=n_q, n_heads=n_heads, layer=layer, blk=blk,
                          n_blocks=n_blocks),
        grid=(b,),
        in_specs=[
            pl.BlockSpec((1, n_q, d_sb), lambda bi: (bi, 0, 0)),
            pl.BlockSpec((1, LANES, d_sb), lambda bi: (bi, 0, 0)),
            pl.BlockSpec((1, LANES, d_sb), lambda bi: (bi, 0, 0)),
            pl.BlockSpec(memory_space=pl.ANY),
            pl.BlockSpec(memory_space=pl.ANY),
            pl.BlockSpec(tri.shape, lambda bi: (0, 0)),
            pl.BlockSpec((1, d_sb), lambda bi: (0, 0)),
        ],
        out_specs=pl.BlockSpec((1, n_q, d_sb), lambda bi: (bi, 0, 0)),
        out_shape=jax.ShapeDtypeStruct((b, n_q, d_sb), BF16),
        scratch_shapes=[pltpu.VMEM((2, d_sb, blk), cache_k.dtype), pltpu.VMEM((2, d_sb, blk), cache_v.dtype),
                        pltpu.SemaphoreType.DMA((2, 2)),
                        pltpu.VMEM((rows, d_sb), BF16), pltpu.VMEM((rows, d_sb), F32),
                        pltpu.VMEM((rows, 1), F32)],
        compiler_params=_params("arbitrary"),
        name="sb_sample",
    )(p_sb, k_new, v_new, cache_k, cache_v, tri, gain)


def _stack(x, first):
    return jnp.concatenate([jnp.where(first, x, 0.0), jnp.where(first, 0.0, x)], axis=0)


def _each(f, *cols):
    return [f(*xs) for xs in zip(*cols)]


def _sum_sel(x, sel, pieces):
    parts = _split2(x) if pieces == 2 else _split3(x)
    n = x.shape[0]
    out = _dg(jnp.concatenate(parts, axis=0), sel, _NN)
    return sum(out[i * n:(i + 1) * n] for i in range(1, len(parts))) + out[:n]


def _unit_lower_inverses(lows, c):
    n = lows[0].shape[0]
    r = lax.broadcasted_iota(jnp.int32, (n, n), 0)
    col = lax.broadcasted_iota(jnp.int32, (n, n), 1)
    eye = jnp.where(r == col, 1.0, 0.0)
    invs = [eye for _ in lows]
    for lvl in range(_log2(c)):
        same = (r >> (lvl + 1)) == (col >> (lvl + 1))
        lower_left = same & (((r >> lvl) & 1) == 1) & (((col >> lvl) & 1) == 0)
        offs = [jnp.where(lower_left, low, 0.0) for low in lows]
        if lvl == 0:
            invs = [eye + off for off in offs]
            continue
        left = _each(lambda inv, off: _dg(inv, off, _NN), invs, offs)
        invs = _each(lambda inv, x: inv + _dg(x, inv, _NN), invs, left)
    return invs


def _rwkv_mix_kernel(p_ref, shift_ref, mu_ref, w0_ref, w2_ref, a0_ref, a2_ref, g2_ref,
                     kk_ref, ka_ref, rk_ref, lnw_ref, lnb_ref, s0_ref, o_ref, sT_ref, s_ref, prev_ref,
                     *, c, t_valid, d_rw, wd, wa):
    ci = pl.program_id(1)

    @pl.when(ci == 0)
    def _():
        s_ref[...] = s0_ref[0]
        prev_ref[...] = shift_ref[0]

    @pl.when(ci * c < t_valid)
    def _():
        p = p_ref[0]
        row = lax.broadcasted_iota(jnp.int32, (c, 1), 0)
        prev = jnp.where(row == 0, prev_ref[...], pltpu.roll(p, 1, 0))
        prev_ref[...] = p[c - 1:c, :]
        xs = p + (prev - p) * mu_ref[...]
        o3 = 3 * d_rw
        dw = xs[:, o3:o3 + wd]
        da = xs[:, o3 + wd:o3 + wd + wa]
        dg = xs[:, o3 + wd + wa:]
        w_pre = w0_ref[...] + _dg(jnp.tanh(dw), w2_ref[...], _NN)
        w_log = -(jnp.maximum(-w_pre, 0.0) + jnp.log1p(jnp.exp(-jnp.abs(w_pre)))) - 0.5
        streams = (xs[:, :d_rw], xs[:, d_rw:2 * d_rw], xs[:, 2 * d_rw:o3],
                   -jnp.exp(w_log),
                   jax.nn.sigmoid(a0_ref[...] + _dg(da, a2_ref[...], _NN)),
                   _dg(jax.nn.sigmoid(dg), g2_ref[...], _NN))
        _rwkv_scan_chunk(ci, streams, kk_ref, ka_ref, rk_ref, lnw_ref, lnb_ref, o_ref, s_ref,
                         c=c, t_valid=t_valid, group=d_rw // LANES)

    @pl.when(ci * c >= t_valid)
    def _():
        o_ref[...] = jnp.zeros_like(o_ref)

    @pl.when(ci == pl.num_programs(1) - 1)
    def _():
        sT_ref[0] = s_ref[...]


def _rwkv_scan_chunk(ci, streams, kk_ref, ka_ref, rk_ref, lnw_ref, lnb_ref, o_ref, s_ref, *,
                     c, t_valid, group):
    lanes = [slice(p * LANES, (p + 1) * LANES) for p in range(group)]
    take = lambda x: [x[:, sl] for sl in lanes]
    par = lambda ref: [ref[:, sl] for sl in lanes]
    r_all, k_all, v_all, lw_all, al_all, g_all = streams
    n = 2 * c
    first = lax.broadcasted_iota(jnp.int32, (1, LANES), 1) < HEAD_DIM
    valid = (ci * c + lax.broadcasted_iota(jnp.int32, (c, 1), 0)) < t_valid
    pair_sum = _head_pair_sum_matrix()
    pair_mean = _head_pair_sum_matrix(1.0 / HEAD_DIM)
    t_i = lax.broadcasted_iota(jnp.int32, (c, c), 0)
    s_i = lax.broadcasted_iota(jnp.int32, (c, c), 1)
    upto = jnp.where(s_i <= t_i, 1.0, 0.0).astype(BF16)
    tt = lax.broadcasted_iota(jnp.int32, (n, n), 0) & (c - 1)
    ss = lax.broadcasted_iota(jnp.int32, (n, n), 1) & (c - 1)
    strict, incl = ss < tt, ss <= tt
    diag = (lax.broadcasted_iota(jnp.int32, (LANES, LANES), 0)
            == lax.broadcasted_iota(jnp.int32, (LANES, LANES), 1))

    r = take(r_all)
    k = [jnp.where(valid, x, 0.0) for x in take(k_all)]
    v = [jnp.where(valid, x, 0.0) for x in take(v_all)]
    lw = [jnp.where(valid, x, 0.0) for x in take(lw_all)]
    al = take(al_all)
    g = take(g_all)

    kk = _each(lambda x, w: x * w, k, par(kk_ref))
    norm = [jnp.sqrt(_sum_sel(x * x, pair_sum, 2)) for x in kk]
    kk = _each(lambda x, nr: x / jnp.maximum(nr, KK_EPS), kk, norm)
    k_mod = _each(lambda x, a_, w: x * (1.0 + (a_ - 1.0) * w), k, al, par(ka_ref))
    b = _each(lambda x, a_: x * a_, kk, al)

    def cumsum(x):
        out = _dg(upto, jnp.concatenate(_split3(x), axis=1), _NN)
        return out[:, :LANES] + out[:, LANES:2 * LANES] + out[:, 2 * LANES:]

    cum = [cumsum(x) for x in lw]
    cum_end = [x[c - 1:c, :] for x in cum]
    dec_out = [jnp.exp(-x) for x in cum]
    dec_rest = _each(lambda e, x: jnp.exp(e - x), cum_end, cum)
    a_t = _each(lambda x, cu, l: _stack(-x * jnp.exp(cu - l), first), kk, cum, lw)
    r_t = _each(lambda x, cu: _stack(x * jnp.exp(cu), first), r, cum)
    b_t = _each(lambda x, d: _stack(x * d, first), b, dec_out)
    k_t = _each(lambda x, d: _stack(x * d, first), k_mod, dec_out)
    b_h = _each(lambda x, d: _stack(x * d, first), b, dec_rest)
    k_h = _each(lambda x, d: _stack(x * d, first), k_mod, dec_rest)
    v_s = [_stack(x, first) for x in v]

    prod = _each(lambda a_, r_, b_, k_: _dg(jnp.concatenate([a_, r_], axis=0),
                                            jnp.concatenate([b_, k_], axis=0), _NT), a_t, r_t, b_t, k_t)
    l_ab = [jnp.where(strict, x[:n, :n], 0.0) for x in prod]
    l_ak = [jnp.where(strict, x[:n, n:], 0.0) for x in prod]
    m_r = [jnp.where(jnp.concatenate([incl, incl], axis=1), x[n:], 0.0) for x in prod]

    solve = _unit_lower_inverses(l_ab, c)
    lv = _each(lambda l_, v_: _dg(l_, v_, _NN), l_ak, v_s)
    au = _each(lambda t_, a_, x: _dg(t_, jnp.concatenate([a_, x], axis=1), _NN), solve, a_t, lv)
    rhs = _each(lambda x, v_: jnp.concatenate(
        [x, jnp.concatenate([jnp.zeros_like(v_), v_], axis=1)], axis=0), au, v_s)
    ry = _each(lambda m, x: _dg(m, x, _NN), m_r, rhs)
    gh = _each(lambda b_, k_, x: _dg(jnp.concatenate([b_, k_], axis=0), x, _TN), b_h, k_h, rhs)

    lhs = _each(lambda r_, x, y, e: jnp.concatenate(
        [r_ + x[:, :LANES], jnp.where(diag, jnp.exp(e), 0.0) + y[:, :LANES]], axis=0), r_t, ry, gh, cum_end)
    step = [_dg3(x, s_ref[p]) for p, x in enumerate(lhs)]
    for p in range(group):
        s_ref[p] = step[p][n:] + gh[p][:, LANES:]
    y = _each(lambda st, x: st[:c] + st[c:n] + x[:c, LANES:] + x[c:, LANES:], step, ry)

    mean = [_sum_sel(x, pair_mean, 2) for x in y]
    d = _each(lambda x, m: x - m, y, mean)
    var = [_sum_sel(x * x, pair_mean, 2) for x in d]
    bonus = _each(lambda r_, k_, w: _sum_sel(r_ * k_ * w, pair_sum, 2), r, k_mod, par(rk_ref))
    for p, sl in enumerate(lanes):
        yn = d[p] * lax.rsqrt(var[p] + GN_EPS) * lnw_ref[:, sl] + lnb_ref[:, sl]
        o_ref[0, :, sl] = ((yn + bonus[p] * v[p]) * g[p]).astype(o_ref.dtype)


def _rwkv_mix(p, pw, shift0, mu, w0, w2, a0, a2, g2, k_k, k_a, r_k, ln_w, ln_b, s0, t_valid):
    b, t_in, _ = p.shape
    d_rw = w0.shape[1]
    pairs = d_rw // LANES
    c = SCAN_CHUNK
    t = -(-t_in // c) * c
    if t != t_in:
        p = jnp.pad(p, ((0, 0), (0, t - t_in), (0, 0)))
    full = lambda a: pl.BlockSpec(a.shape, lambda bi, ci: (0,) * a.ndim)
    st_spec = pl.BlockSpec((1, pairs, LANES, LANES), lambda bi, ci: (bi, 0, 0, 0))
    params = (mu, w0, w2, a0, a2, g2, k_k, k_a, r_k, ln_w, ln_b)
    o, s_t = pl.pallas_call(
        functools.partial(_rwkv_mix_kernel, c=c, t_valid=t_valid, d_rw=d_rw, wd=w2.shape[0],
                          wa=a2.shape[0]),
        grid=(b, t // c),
        in_specs=[pl.BlockSpec((1, c, pw), lambda bi, ci: (bi, ci, 0)),
                  pl.BlockSpec((1, 1, pw), lambda bi, ci: (bi, 0, 0))]
                 + [full(a) for a in params] + [st_spec],
        out_specs=[pl.BlockSpec((1, c, d_rw), lambda bi, ci: (bi, ci, 0)), st_spec],
        out_shape=[jax.ShapeDtypeStruct((b, t, d_rw), BF16),
                   jax.ShapeDtypeStruct((b, pairs, LANES, LANES), F32)],
        scratch_shapes=[pltpu.VMEM((pairs, LANES, LANES), F32), pltpu.VMEM((1, pw), F32)],
        compiler_params=_params("parallel", "arbitrary"),
        name="rwkv_mix",
    )(p, shift0, *params, s0)
    return o[:, :t_in], s_t


def _state_to_pairs(s):
    b, h, e, _ = s.shape
    st = jnp.swapaxes(s, -1, -2).reshape(b, h // 2, 2, e, e)
    z = jnp.zeros_like(st[:, :, 0])
    top = jnp.concatenate([st[:, :, 0], z], axis=-1)
    bot = jnp.concatenate([z, st[:, :, 1]], axis=-1)
    return jnp.concatenate([top, bot], axis=-2)


def _pairs_to_state(sp):
    b, pairs = sp.shape[:2]
    e = HEAD_DIM
    blocks = jnp.stack([sp[:, :, :e, :e], sp[:, :, e:, e:]], axis=2)
    return jnp.swapaxes(blocks, -1, -2).reshape(b, 2 * pairs, e, e)


def _pad_cols(x, width):
    return jnp.pad(x, [(0, 0)] * (x.ndim - 1) + [(0, width - x.shape[-1])])


def _rw_layout(x, d_rw, ranks, widths):
    parts = [x[..., :3 * d_rw]]
    o = 3 * d_rw
    for rank, width in zip(ranks, widths):
        parts.append(_pad_cols(x[..., o:o + rank], width))
        o += rank
    return jnp.concatenate(parts, axis=-1)


def _rw_unlayout(x, d_rw, ranks, widths):
    parts = [x[..., :3 * d_rw]]
    o = 3 * d_rw
    for rank, width in zip(ranks, widths):
        parts.append(x[..., o:o + rank])
        o += width
    return jnp.concatenate(parts, axis=-1)


def _layer(h, lw, cache, s0_pairs, shift0, t_valid, tri):
    b, t, d = h.shape
    d_sb = lw["sb_gain"].shape[-1]
    x = h.reshape(b * t, d)
    x = _ffn(x, lw["ffn1_pre"], lw["ffn1_post"], lw["ffn1_gate"], lw["ffn1_up"], lw["ffn1_down"])
    p = _norm_matmul(x, lw["mix_pre"], lw["w_in"], MIX_IN_COLS).reshape(b, t, -1)
    rw_cols = p.shape[-1] - 3 * d_sb
    if cache is None:
        o_sb = _sb_prompt(p, rw_cols, d_sb, lw["sb_gain"], tri)
    else:
        o_sb = _sb_sample(p[:, :, rw_cols:], cache[0], cache[1], cache[2], lw["sb_gain"], tri)
    o_rw, s_t = _rwkv_mix(p, rw_cols, shift0, lw["mu"], lw["w0"], lw["w2"], lw["a0"], lw["a2"], lw["g2"],
                          lw["k_k"], lw["k_a"], lw["r_k"], lw["ln_w"], lw["ln_b"], s0_pairs, t_valid)
    x = _mix_out(o_sb.reshape(b * t, -1), o_rw.reshape(b * t, -1), lw["w_out_sb"], lw["w_out_rw"],
                 x, lw["mix_post"])
    x = _ffn(x, lw["ffn2_pre"], lw["ffn2_post"], lw["ffn2_gate"], lw["ffn2_up"], lw["ffn2_down"])
    k_new = p[:, :t_valid, rw_cols + d_sb:rw_cols + 2 * d_sb]
    v_new = p[:, :t_valid, rw_cols + 2 * d_sb:]
    return x.reshape(b, t, d), k_new, v_new, s_t, p[:, t_valid - 1, :rw_cols]


def kernel(x_prompt, x_sample, cache_sb_k, cache_sb_v, state_rwkv_S, state_rwkv_shift, meta_tokens, ffn1_norm_pre, ffn1_norm_post, ffn1_w_gate, ffn1_w_up, ffn1_w_down, mix_norm_pre, mix_norm_post, w_in, sb_out_gain, rwkv_mu, rwkv_w0, rwkv_w2, rwkv_a0, rwkv_a2, rwkv_g2, rwkv_k_k, rwkv_k_a, rwkv_r_k, rwkv_ln_w, rwkv_ln_b, w_out, ffn2_norm_pre, ffn2_norm_post, ffn2_w_gate, ffn2_w_up, ffn2_w_down):
    depth, d = ffn1_norm_pre.shape
    bp, seq, _ = x_prompt.shape
    bs, dec_seq, _ = x_sample.shape
    n_meta = meta_tokens.shape[0]
    h_sb = sb_out_gain.shape[1]
    d_sb = h_sb * HEAD_DIM
    d_rw = rwkv_w0.shape[1]
    h_rw = d_rw // HEAD_DIM
    ranks = (rwkv_w2.shape[1], rwkv_a2.shape[1], rwkv_g2.shape[1])
    widths = tuple(-(-r // LANES) * LANES for r in ranks)
    p_rw_cols = 3 * d_rw + sum(ranks)
    past = cache_sb_k.shape[2]

    t_valid = n_meta + seq
    t_pad = -(-t_valid // ATTN_BLOCK) * ATTN_BLOCK
    meta = jnp.broadcast_to(meta_tokens[None].astype(x_prompt.dtype), (bp, n_meta, d))
    hp = jnp.concatenate([meta, x_prompt, jnp.zeros((bp, t_pad - t_valid, d), x_prompt.dtype)], axis=1)
    hs = x_sample

    tri = _strict_upper(ATTN_BLOCK)
    feature_major = lambda c: jnp.transpose(c, (0, 1, 3, 4, 2)).reshape(depth, bs, d_sb, past)
    cache_kt, cache_vt = feature_major(cache_sb_k), feature_major(cache_sb_v)
    row = lambda x: x.reshape(1, -1)
    pad_rows = lambda w, width: jnp.pad(w, ((0, width - w.shape[0]), (0, 0)))
    w_in_rw = _rw_layout(w_in[:, :, 3 * d_sb:], d_rw, ranks, widths)
    w_in_all = jnp.concatenate([w_in_rw, w_in[:, :, :3 * d_sb]], axis=-1).astype(BF16)
    ffn_w = [_to_bf16(w) for w in (ffn1_w_gate, ffn1_w_up, ffn1_w_down, ffn2_w_gate, ffn2_w_up, ffn2_w_down)]
    mu_all = _rw_layout(rwkv_mu, d_rw, ranks, widths)

    outs = [[] for _ in range(8)]
    s0_p = jnp.zeros((bp, h_rw // 2, LANES, LANES), F32)
    shift0_p = jnp.zeros((bp, 1, w_in_rw.shape[-1]), F32)
    for l in range(depth):
        lw = dict(
            ffn1_pre=row(ffn1_norm_pre[l]), ffn1_post=row(ffn1_norm_post[l]),
            ffn1_gate=ffn_w[0][l], ffn1_up=ffn_w[1][l], ffn1_down=ffn_w[2][l],
            ffn2_pre=row(ffn2_norm_pre[l]), ffn2_post=row(ffn2_norm_post[l]),
            ffn2_gate=ffn_w[3][l], ffn2_up=ffn_w[4][l], ffn2_down=ffn_w[5][l],
            mix_pre=row(mix_norm_pre[l]), mix_post=row(mix_norm_post[l]),
            w_in=w_in_all[l], sb_gain=row(sb_out_gain[l]),
            mu=row(mu_all[l]), w0=row(rwkv_w0[l]), a0=row(rwkv_a0[l]),
            w2=pad_rows(rwkv_w2[l], widths[0]).astype(BF16),
            a2=pad_rows(rwkv_a2[l], widths[1]).astype(BF16),
            g2=pad_rows(rwkv_g2[l], widths[2]).astype(BF16),
            k_k=row(rwkv_k_k[l]), k_a=row(rwkv_k_a[l]), r_k=row(rwkv_r_k[l]),
            ln_w=row(rwkv_ln_w[l]), ln_b=row(rwkv_ln_b[l]),
            w_out_sb=w_out[l, :d_sb].astype(BF16), w_out_rw=w_out[l, d_sb:].astype(BF16),
        )
        hp, kp, vp, sp, shp = _layer(hp, lw, None, s0_p, shift0_p, t_valid, tri)
        cache = (cache_kt, cache_vt, l)
        shift0_s = _rw_layout(state_rwkv_shift[l], d_rw, ranks, widths)[:, None, :]
        hs, ks, vs, ss, shs = _layer(hs, lw, cache, _state_to_pairs(state_rwkv_S[l].astype(F32)),
                                     shift0_s, dec_seq, tri)
        per_layer = (
            kp.reshape(bp, t_valid, h_sb, HEAD_DIM), vp.reshape(bp, t_valid, h_sb, HEAD_DIM),
            _pairs_to_state(sp).astype(state_rwkv_S.dtype),
            _rw_unlayout(shp, d_rw, ranks, widths).astype(state_rwkv_shift.dtype),
            ks.reshape(bs, dec_seq, h_sb, HEAD_DIM), vs.reshape(bs, dec_seq, h_sb, HEAD_DIM),
            _pairs_to_state(ss).astype(state_rwkv_S.dtype),
            _rw_unlayout(shs, d_rw, ranks, widths).astype(state_rwkv_shift.dtype),
        )
        for acc, val in zip(outs, per_layer):
            acc.append(val)
    assert p_rw_cols == state_rwkv_shift.shape[-1]
    y_prompt = hp[:, n_meta:t_valid]
    return (y_prompt, hs) + tuple(jnp.stack(o) for o in outs)
```

```python
import functools

import jax
import jax.numpy as jnp
from jax import lax
from jax.experimental import pallas as pl
from jax.experimental.pallas import tpu as pltpu

F32 = jnp.float32
BF16 = jnp.bfloat16

HEAD_DIM = 64
LANES = 128
NORM_EPS = 1e-6
GN_EPS = 64e-5
KK_EPS = 1e-12
VMEM_LIMIT_BYTES = 56 * 2 ** 20

ATTN_BLOCK = 256
ATTN_GROUP = 2
ATTN_UNROLL = 2
CACHE_BLOCK = 256
DEAD_LOG = -104.0
MIX_IN_COLS = 1664
SCAN_CHUNK = 64
ROW_TILE = 512
FF_TILE = 512
CAST_BLOCK_BYTES = 8 * 2 ** 20


def _params(*sem):
    return pltpu.CompilerParams(dimension_semantics=sem, vmem_limit_bytes=VMEM_LIMIT_BYTES)


def _tile(n, pref, mult=16):
    t = min(pref, n)
    t -= t % mult
    while t > mult and n % t:
        t -= mult
    assert t >= mult and n % t == 0, (n, pref)
    return t


def _log2(n):
    k = n.bit_length() - 1
    assert 1 << k == n, n
    return k


_NN = (((1,), (0,)), ((), ()))
_NT = (((1,), (1,)), ((), ()))
_TN = (((0,), (0,)), ((), ()))


def _dg(a, b, dims):
    return lax.dot_general(a.astype(BF16), b.astype(BF16), dims, preferred_element_type=F32)


def _split2(x):
    hi = x.astype(BF16)
    lo = (x - hi.astype(F32)).astype(BF16)
    return hi, lo


def _split3(x):
    hi = x.astype(BF16)
    r = x - hi.astype(F32)
    mid = r.astype(BF16)
    lo = (r - mid.astype(F32)).astype(BF16)
    return hi, mid, lo


def _dg3(a, b, dims=_NN):
    ah, al = _split2(a)
    bh, bl = _split2(b)
    return _dg(ah, bh, dims) + (_dg(ah, bl, dims) + _dg(al, bh, dims))


def _rms(x, eps=NORM_EPS):
    return x * lax.rsqrt(jnp.mean(x * x, axis=-1, keepdims=True) + eps)


def _head_pair_sum_matrix(scale=1.0):
    r = lax.broadcasted_iota(jnp.int32, (LANES, LANES), 0) >> _log2(HEAD_DIM)
    c = lax.broadcasted_iota(jnp.int32, (LANES, LANES), 1) >> _log2(HEAD_DIM)
    return jnp.where(r == c, scale, 0.0).astype(BF16)


def _cast_kernel(x_ref, o_ref):
    o_ref[...] = x_ref[...].astype(o_ref.dtype)


def _to_bf16(x):
    flat = x.reshape(-1, x.shape[-1])
    n, w = flat.shape
    tm = _tile(n, max(16, CAST_BLOCK_BYTES // (4 * w)))
    out = pl.pallas_call(
        _cast_kernel,
        grid=(n // tm,),
        in_specs=[pl.BlockSpec((tm, w), lambda i: (i, 0))],
        out_specs=pl.BlockSpec((tm, w), lambda i: (i, 0)),
        out_shape=jax.ShapeDtypeStruct((n, w), BF16),
        compiler_params=_params("parallel"),
        name="to_bf16",
    )(flat)
    return out.reshape(x.shape)


def _ffn_kernel(x_ref, gpre_ref, gpost_ref, wg_ref, wu_ref, wd_ref, o_ref, u_ref, acc_ref):
    f = pl.program_id(1)

    @pl.when(f == 0)
    def _():
        u_ref[...] = (_rms(x_ref[...]) * gpre_ref[...]).astype(BF16)
        acc_ref[...] = jnp.zeros_like(acc_ref)

    u = u_ref[...]
    gate = jnp.dot(u, wg_ref[...], preferred_element_type=F32)
    up = jnp.dot(u, wu_ref[...], preferred_element_type=F32)
    act = (gate * jax.nn.sigmoid(gate) * up).astype(BF16)
    acc_ref[...] += jnp.dot(act, wd_ref[...], preferred_element_type=F32)

    @pl.when(f == pl.num_programs(1) - 1)
    def _():
        o_ref[...] = x_ref[...] + 0.5 * (_rms(acc_ref[...]) * gpost_ref[...])


def _ffn(x, g_pre, g_post, w_gate, w_up, w_down):
    n, d = x.shape
    d_ff = w_gate.shape[1]
    tm = _tile(n, ROW_TILE)
    tf = _tile(d_ff, FF_TILE, LANES)
    return pl.pallas_call(
        _ffn_kernel,
        grid=(n // tm, d_ff // tf),
        in_specs=[
            pl.BlockSpec((tm, d), lambda i, f: (i, 0)),
            pl.BlockSpec((1, d), lambda i, f: (0, 0)),
            pl.BlockSpec((1, d), lambda i, f: (0, 0)),
            pl.BlockSpec((d, tf), lambda i, f: (0, f)),
            pl.BlockSpec((d, tf), lambda i, f: (0, f)),
            pl.BlockSpec((tf, d), lambda i, f: (f, 0)),
        ],
        out_specs=pl.BlockSpec((tm, d), lambda i, f: (i, 0)),
        out_shape=jax.ShapeDtypeStruct((n, d), F32),
        scratch_shapes=[pltpu.VMEM((tm, d), BF16), pltpu.VMEM((tm, d), F32)],
        compiler_params=_params("parallel", "arbitrary"),
        name="ffn_half",
    )(x, g_pre, g_post, w_gate, w_up, w_down)


def _norm_matmul_kernel(x_ref, g_ref, w_ref, o_ref):
    u = (_rms(x_ref[...]) * g_ref[...]).astype(BF16)
    o_ref[...] = jnp.dot(u, w_ref[...], preferred_element_type=F32)


def _norm_matmul(x, g, w, col_tile):
    n, d = x.shape
    cols = w.shape[1]
    tm = _tile(n, ROW_TILE)
    tn = _tile(cols, col_tile, LANES)
    return pl.pallas_call(
        _norm_matmul_kernel,
        grid=(cols // tn, n // tm),
        in_specs=[
            pl.BlockSpec((tm, d), lambda j, i: (i, 0)),
            pl.BlockSpec((1, d), lambda j, i: (0, 0)),
            pl.BlockSpec((d, tn), lambda j, i: (0, j)),
        ],
        out_specs=pl.BlockSpec((tm, tn), lambda j, i: (i, j)),
        out_shape=jax.ShapeDtypeStruct((n, cols), F32),
        compiler_params=_params("parallel", "parallel"),
        name="mix_in",
    )(x, g, w)


def _mix_out_kernel(osb_ref, orw_ref, wsb_ref, wrw_ref, h_ref, g_ref, o_ref):
    o = jnp.dot(osb_ref[...], wsb_ref[...], preferred_element_type=F32)
    o = o + jnp.dot(orw_ref[...], wrw_ref[...], preferred_element_type=F32)
    o_ref[...] = h_ref[...] + _rms(o) * g_ref[...]


def _mix_out(o_sb, o_rw, w_sb, w_rw, h, g):
    n, d = h.shape
    tm = _tile(n, ROW_TILE // 2)
    return pl.pallas_call(
        _mix_out_kernel,
        grid=(n // tm,),
        in_specs=[
            pl.BlockSpec((tm, o_sb.shape[1]), lambda i: (i, 0)),
            pl.BlockSpec((tm, o_rw.shape[1]), lambda i: (i, 0)),
            pl.BlockSpec(w_sb.shape, lambda i: (0, 0)),
            pl.BlockSpec(w_rw.shape, lambda i: (0, 0)),
            pl.BlockSpec((tm, d), lambda i: (i, 0)),
            pl.BlockSpec((1, d), lambda i: (0, 0)),
        ],
        out_specs=pl.BlockSpec((tm, d), lambda i: (i, 0)),
        out_shape=jax.ShapeDtypeStruct((n, d), F32),
        compiler_params=_params("parallel"),
        name="mix_out",
    )(o_sb, o_rw, w_sb, w_rw, h, g)


def _strict_upper(n):
    r = lax.broadcasted_iota(jnp.int32, (n, n), 0)
    c = lax.broadcasted_iota(jnp.int32, (n, n), 1)
    return jnp.where(r > c, 1.0, 0.0).astype(BF16)


def _split_top16(x):
    bits = lax.bitcast_convert_type(x, jnp.uint32) & jnp.uint32(0xFFFF0000)
    hi = lax.bitcast_convert_type(bits, F32)
    return hi.astype(BF16), (x - hi).astype(BF16)


def _sb_sweep(qs, kv_blocks, tri, runs, masks, feature_major=False):
    jobs = []
    for block in kv_blocks:
        for c, (q, mask) in enumerate(zip(qs, masks)):
            k_blk, v_blk = block[c] if isinstance(block, list) else block
            z = _dg(q, k_blk, _NN if feature_major else _NT)
            nz = -z
            soft = jnp.log(1.0 + jnp.exp(jnp.minimum(z, nz)))
            stay = jnp.minimum(nz, 0.0) - soft
            log_sig = stay + z
            if mask is not None:
                stay = jnp.where(mask, stay, 0.0)
            stacked = jnp.concatenate(_split_top16(stay), axis=0)
            jobs.append((c, v_blk, log_sig, stacked, stay[:, :1], mask))
    between = [jnp.dot(job[3], tri, preferred_element_type=F32) for job in jobs]
    runs = list(runs)
    outs = [None] * len(qs)
    for (c, v_blk, log_sig, _, stay0, mask), btw in zip(jobs, between):
        rows = log_sig.shape[0]
        right = btw[:rows] + btw[rows:]
        a = jnp.exp(log_sig + right)
        if mask is not None:
            a = jnp.where(mask, a, 0.0)
        o = jnp.exp(runs[c]) * _dg(a, v_blk, _NT if feature_major else _NN)
        outs[c] = o if outs[c] is None else outs[c] + o
        runs[c] = runs[c] + (right[:, :1] + stay0)
    return outs, runs


def _head_rms(o, gain, first):
    sq = o * o
    s0 = jnp.sum(jnp.where(first, sq, 0.0), axis=-1, keepdims=True)
    s1 = jnp.sum(jnp.where(first, 0.0, sq), axis=-1, keepdims=True)
    ms = jnp.where(first, s0, s1) * (1.0 / HEAD_DIM)
    return o * lax.rsqrt(ms + NORM_EPS) * gain


def _sb_prompt_kernel(q_ref, k_ref, v_ref, tri_ref, gain_ref, o_ref, acc_ref, run_ref, *, blk, group):
    qi = pl.program_id(2)
    first = lax.broadcasted_iota(jnp.int32, (1, LANES), 1) < HEAD_DIM
    lanes = [slice(p * LANES, (p + 1) * LANES) for p in range(group)]
    q_heads = []
    for sl in lanes:
        q = q_ref[0, :, sl] * (HEAD_DIM ** -0.5)
        q_heads += [jnp.where(first, q, 0.0).astype(BF16), jnp.where(first, 0.0, q).astype(BF16)]
    n_chains = len(q_heads)
    tri = tri_ref[...]
    causal = (lax.broadcasted_iota(jnp.int32, (blk, blk), 1)
              < lax.broadcasted_iota(jnp.int32, (blk, blk), 0))

    acc_ref[...] = jnp.zeros_like(acc_ref)
    run_ref[...] = jnp.zeros_like(run_ref)

    def sweep(kb, n_blocks, mask):
        kv = []
        for j in range(n_blocks):
            start = pl.multiple_of((kb - j) * blk, blk)
            k_all = k_ref[0, pl.ds(start, blk), :].astype(BF16)
            v_all = v_ref[0, pl.ds(start, blk), :].astype(BF16)
            kv.append([(k_all[:, lanes[c // 2]], v_all[:, lanes[c // 2]]) for c in range(n_chains)])
        outs, runs = _sb_sweep(q_heads, kv, tri, [run_ref[c] for c in range(n_chains)],
                               [mask] * n_chains)
        for c in range(n_chains):
            run_ref[c] = runs[c]
            acc_ref[c] += outs[c]

    sweep(qi, 1, causal)

    def alive():
        return jnp.max(run_ref[...]) > DEAD_LOG

    rest = qi % ATTN_UNROLL
    singles = jnp.where(rest > 0, rest, jnp.minimum(qi, ATTN_UNROLL))

    def one(carry):
        i, _ = carry
        sweep(qi - 1 - i, 1, None)
        return i + 1, alive()

    _, live = lax.while_loop(lambda c: (c[0] < singles) & c[1], one, (jnp.int32(0), qi >= 0))
    groups = (qi - singles) // ATTN_UNROLL

    def many(carry):
        g, _ = carry
        sweep(qi - 1 - singles - g * ATTN_UNROLL, ATTN_UNROLL, None)
        return g + 1, alive()

    lax.while_loop(lambda c: (c[0] < groups) & c[1], many, (jnp.int32(0), live))
    for p, sl in enumerate(lanes):
        o = jnp.where(first, acc_ref[2 * p], acc_ref[2 * p + 1])
        o_ref[0, :, sl] = _head_rms(o, gain_ref[0, :, sl], first).astype(o_ref.dtype)


def _sb_prompt(p, col0, d_sb, gain, tri):
    b, t, _ = p.shape
    group = ATTN_GROUP
    width = group * LANES
    blk = ATTN_BLOCK
    assert t % blk == 0 and d_sb % width == 0 and col0 % width == 0
    n_groups = d_sb // width
    c0 = col0 // width
    return pl.pallas_call(
        functools.partial(_sb_prompt_kernel, blk=blk, group=group),
        grid=(b, n_groups, t // blk),
        in_specs=[
            pl.BlockSpec((1, blk, width), lambda bi, g, qi: (bi, qi, c0 + g)),
            pl.BlockSpec((1, t, width), lambda bi, g, qi: (bi, 0, c0 + n_groups + g)),
            pl.BlockSpec((1, t, width), lambda bi, g, qi: (bi, 0, c0 + 2 * n_groups + g)),
            pl.BlockSpec((blk, blk), lambda bi, g, qi: (0, 0)),
            pl.BlockSpec((1, 1, width), lambda bi, g, qi: (g, 0, 0)),
        ],
        out_specs=pl.BlockSpec((1, blk, width), lambda bi, g, qi: (bi, qi, g)),
        out_shape=jax.ShapeDtypeStruct((b, t, d_sb), BF16),
        scratch_shapes=[pltpu.VMEM((2 * group, blk, LANES), F32), pltpu.VMEM((2 * group, blk, 1), F32)],
        compiler_params=_params("parallel", "parallel", "arbitrary"),
        name="sb_prompt",
    )(p, p, p, tri, gain.reshape(n_groups, 1, width))


def _sb_sample_kernel(q_ref, kn_ref, vn_ref, kc_hbm, vc_hbm, tri_ref, gain_ref, o_ref,
                      kbuf, vbuf, sem, qs_ref, acc_ref, run_ref, *, n_q, n_heads, layer, blk, n_blocks):
    bi = pl.program_id(0)
    rows = n_heads * n_q
    d_sb = n_heads * HEAD_DIM
    past = n_blocks * blk

    def fetch(i, slot):
        start = pl.multiple_of(past - (i + 1) * blk, blk)
        src = lambda hbm: hbm.at[layer, bi, :, pl.ds(start, blk)]
        return (pltpu.make_async_copy(src(kc_hbm), kbuf.at[slot], sem.at[slot, 0]),
                pltpu.make_async_copy(src(vc_hbm), vbuf.at[slot], sem.at[slot, 1]))

    for cp in fetch(0, 0):
        cp.start()

    row_head = lax.broadcasted_iota(jnp.int32, (rows, 1), 0) >> _log2(n_q)
    lane_head = lax.broadcasted_iota(jnp.int32, (1, d_sb), 1) >> _log2(HEAD_DIM)
    q = jnp.tile(q_ref[0] * (HEAD_DIM ** -0.5), (n_heads, 1))
    qs = jnp.where(row_head == lane_head, q, 0.0).astype(BF16)
    qs_ref[...] = qs
    n_new = kn_ref.shape[1]
    q_pos = lax.broadcasted_iota(jnp.int32, (rows, 1), 0) & (n_q - 1)
    mask = lax.broadcasted_iota(jnp.int32, (1, n_new), 1) < q_pos
    outs, runs = _sb_sweep([qs], [(kn_ref[0].astype(BF16), vn_ref[0].astype(BF16))],
                           tri_ref[:n_new, :n_new], [jnp.zeros((rows, 1), F32)], [mask])
    run_ref[...] = runs[0]
    acc_ref[...] = outs[0]

    def alive():
        return jnp.max(run_ref[...]) > DEAD_LOG

    half = rows // 2
    chains = [slice(0, half), slice(half, rows)]
    sub = tri_ref.shape[0]

    def body(carry):
        i, _ = carry
        slot = i & 1
        for cp in fetch(i, slot):
            cp.wait()

        @pl.when(i + 1 < n_blocks)
        def _():
            for cp in fetch(i + 1, 1 - slot):
                cp.start()

        kv = [(kbuf[slot, :, s:s + sub].astype(BF16), vbuf[slot, :, s:s + sub].astype(BF16))
              for s in range(blk - sub, -1, -sub)]
        outs, runs = _sb_sweep([qs_ref[rs, :] for rs in chains], kv, tri_ref[...],
                               [run_ref[rs, :] for rs in chains], [None, None], feature_major=True)
        for rs, o, run in zip(chains, outs, runs):
            run_ref[rs, :] = run
            acc_ref[rs, :] += o
        return i + 1, alive()

    done, _ = lax.while_loop(lambda c: (c[0] < n_blocks) & c[1], body, (jnp.int32(0), alive()))

    @pl.when(done < n_blocks)
    def _():
        for cp in fetch(done, done & 1):
            cp.wait()

    o = jnp.zeros((n_q, d_sb), F32)
    for h in range(n_heads):
        o = o + jnp.where(lane_head == h, acc_ref[h * n_q:(h + 1) * n_q, :], 0.0)
    first = (lax.broadcasted_iota(jnp.int32, (1, LANES), 1) < HEAD_DIM)
    for p in range(d_sb // LANES):
        sl = slice(p * LANES, (p + 1) * LANES)
        o_ref[0, :, sl] = _head_rms(o[:, sl], gain_ref[:, sl], first).astype(o_ref.dtype)


def _sb_sample(p_sb, cache_k, cache_v, layer, gain, tri):
    b, n_q, w = p_sb.shape
    d_sb = w // 3
    n_heads = d_sb // HEAD_DIM
    past = cache_k.shape[3]
    blk = min(CACHE_BLOCK, past)
    assert past % blk == 0 and blk % tri.shape[0] == 0 and n_q <= LANES
    n_blocks = past // blk
    pad = ((0, 0), (0, LANES - n_q), (0, 0))
    k_new = jnp.pad(p_sb[:, :, d_sb:2 * d_sb], pad)
    v_new = jnp.pad(p_sb[:, :, 2 * d_sb:], pad)
    rows = n_heads * n_q
    return pl.pallas_call(
        functools.partial(_sb_sample_kernel, n_q=n_q, n_heads=n_heads, layer=layer, blk=blk,
                          n_blocks=n_blocks),
        grid=(b,),
        in_specs=[
            pl.BlockSpec((1, n_q, d_sb), lambda bi: (bi, 0, 0)),
            pl.BlockSpec((1, LANES, d_sb), lambda bi: (bi, 0, 0)),
            pl.BlockSpec((1, LANES, d_sb), lambda bi: (bi, 0, 0)),
            pl.BlockSpec(memory_space=pl.ANY),
            pl.BlockSpec(memory_space=pl.ANY),
            pl.BlockSpec(tri.shape, lambda bi: (0, 0)),
            pl.BlockSpec((1, d_sb), lambda bi: (0, 0)),
        ],
        out_specs=pl.BlockSpec((1, n_q, d_sb), lambda bi: (bi, 0, 0)),
        out_shape=jax.ShapeDtypeStruct((b, n_q, d_sb), BF16),
        scratch_shapes=[pltpu.VMEM((2, d_sb, blk), cache_k.dtype), pltpu.VMEM((2, d_sb, blk), cache_v.dtype),
                        pltpu.SemaphoreType.DMA((2, 2)),
                        pltpu.VMEM((rows, d_sb), BF16), pltpu.VMEM((rows, d_sb), F32),
                        pltpu.VMEM((rows, 1), F32)],
        compiler_params=_params("arbitrary"),
        name="sb_sample",
    )(p_sb, k_new, v_new, cache_k, cache_v, tri, gain)


def _stack(x, first):
    return jnp.concatenate([jnp.where(first, x, 0.0), jnp.where(first, 0.0, x)], axis=0)


def _each(f, *cols):
    return [f(*xs) for xs in zip(*cols)]


def _sum_sel(x, sel, pieces):
    parts = _split2(x) if pieces == 2 else _split3(x)
    n = x.shape[0]
    out = _dg(jnp.concatenate(parts, axis=0), sel, _NN)
    return sum(out[i * n:(i + 1) * n] for i in range(1, len(parts))) + out[:n]


def _unit_lower_inverses(lows, c):
    n = lows[0].shape[0]
    r = lax.broadcasted_iota(jnp.int32, (n, n), 0)
    col = lax.broadcasted_iota(jnp.int32, (n, n), 1)
    eye = jnp.where(r == col, 1.0, 0.0)
    invs = [eye for _ in lows]
    for lvl in range(_log2(c)):
        same = (r >> (lvl + 1)) == (col >> (lvl + 1))
        lower_left = same & (((r >> lvl) & 1) == 1) & (((col >> lvl) & 1) == 0)
        offs = [jnp.where(lower_left, low, 0.0) for low in lows]
        if lvl == 0:
            invs = [eye + off for off in offs]
            continue
        left = _each(lambda inv, off: _dg(inv, off, _NN), invs, offs)
        invs = _each(lambda inv, x: inv + _dg(x, inv, _NN), invs, left)
    return invs


def _rwkv_mix_kernel(p_ref, shift_ref, mu_ref, w0_ref, w2_ref, a0_ref, a2_ref, g2_ref,
                     kk_ref, ka_ref, rk_ref, lnw_ref, lnb_ref, s0_ref, o_ref, sT_ref, s_ref, prev_ref,
                     *, c, t_valid, d_rw, wd, wa):
    ci = pl.program_id(1)

    @pl.when(ci == 0)
    def _():
        s_ref[...] = s0_ref[0]
        prev_ref[...] = shift_ref[0]

    @pl.when(ci * c < t_valid)
    def _():
        p = p_ref[0]
        row = lax.broadcasted_iota(jnp.int32, (c, 1), 0)
        prev = jnp.where(row == 0, prev_ref[...], pltpu.roll(p, 1, 0))
        prev_ref[...] = p[c - 1:c, :]
        xs = p + (prev - p) * mu_ref[...]
        o3 = 3 * d_rw
        dw = xs[:, o3:o3 + wd]
        da = xs[:, o3 + wd:o3 + wd + wa]
        dg = xs[:, o3 + wd + wa:]
        w_pre = w0_ref[...] + _dg(jnp.tanh(dw), w2_ref[...], _NN)
        w_log = -(jnp.maximum(-w_pre, 0.0) + jnp.log1p(jnp.exp(-jnp.abs(w_pre)))) - 0.5
        streams = (xs[:, :d_rw], xs[:, d_rw:2 * d_rw], xs[:, 2 * d_rw:o3],
                   -jnp.exp(w_log),
                   jax.nn.sigmoid(a0_ref[...] + _dg(da, a2_ref[...], _NN)),
                   _dg(jax.nn.sigmoid(dg), g2_ref[...], _NN))
        _rwkv_scan_chunk(ci, streams, kk_ref, ka_ref, rk_ref, lnw_ref, lnb_ref, o_ref, s_ref,
                         c=c, t_valid=t_valid, group=d_rw // LANES)

    @pl.when(ci * c >= t_valid)
    def _():
        o_ref[...] = jnp.zeros_like(o_ref)

    @pl.when(ci == pl.num_programs(1) - 1)
    def _():
        sT_ref[0] = s_ref[...]


def _rwkv_scan_chunk(ci, streams, kk_ref, ka_ref, rk_ref, lnw_ref, lnb_ref, o_ref, s_ref, *,
                     c, t_valid, group):
    lanes = [slice(p * LANES, (p + 1) * LANES) for p in range(group)]
    take = lambda x: [x[:, sl] for sl in lanes]
    par = lambda ref: [ref[:, sl] for sl in lanes]
    r_all, k_all, v_all, lw_all, al_all, g_all = streams
    n = 2 * c
    first = lax.broadcasted_iota(jnp.int32, (1, LANES), 1) < HEAD_DIM
    valid = (ci * c + lax.broadcasted_iota(jnp.int32, (c, 1), 0)) < t_valid
    pair_sum = _head_pair_sum_matrix()
    pair_mean = _head_pair_sum_matrix(1.0 / HEAD_DIM)
    t_i = lax.broadcasted_iota(jnp.int32, (c, c), 0)
    s_i = lax.broadcasted_iota(jnp.int32, (c, c), 1)
    upto = jnp.where(s_i <= t_i, 1.0, 0.0).astype(BF16)
    tt = lax.broadcasted_iota(jnp.int32, (n, n), 0) & (c - 1)
    ss = lax.broadcasted_iota(jnp.int32, (n, n), 1) & (c - 1)
    strict, incl = ss < tt, ss <= tt
    diag = (lax.broadcasted_iota(jnp.int32, (LANES, LANES), 0)
            == lax.broadcasted_iota(jnp.int32, (LANES, LANES), 1))

    r = take(r_all)
    k = [jnp.where(valid, x, 0.0) for x in take(k_all)]
    v = [jnp.where(valid, x, 0.0) for x in take(v_all)]
    lw = [jnp.where(valid, x, 0.0) for x in take(lw_all)]
    al = take(al_all)
    g = take(g_all)

    kk = _each(lambda x, w: x * w, k, par(kk_ref))
    norm = [jnp.sqrt(_sum_sel(x * x, pair_sum, 2)) for x in kk]
    kk = _each(lambda x, nr: x / jnp.maximum(nr, KK_EPS), kk, norm)
    k_mod = _each(lambda x, a_, w: x * (1.0 + (a_ - 1.0) * w), k, al, par(ka_ref))
    b = _each(lambda x, a_: x * a_, kk, al)

    def cumsum(x):
        out = _dg(upto, jnp.concatenate(_split3(x), axis=1), _NN)
        return out[:, :LANES] + out[:, LANES:2 * LANES] + out[:, 2 * LANES:]

    cum = [cumsum(x) for x in lw]
    cum_end = [x[c - 1:c, :] for x in cum]
    dec_out = [jnp.exp(-x) for x in cum]
    dec_rest = _each(lambda e, x: jnp.exp(e - x), cum_end, cum)
    a_t = _each(lambda x, cu, l: _stack(-x * jnp.exp(cu - l), first), kk, cum, lw)
    r_t = _each(lambda x, cu: _stack(x * jnp.exp(cu), first), r, cum)
    b_t = _each(lambda x, d: _stack(x * d, first), b, dec_out)
    k_t = _each(lambda x, d: _stack(x * d, first), k_mod, dec_out)
    b_h = _each(lambda x, d: _stack(x * d, first), b, dec_rest)
    k_h = _each(lambda x, d: _stack(x * d, first), k_mod, dec_rest)
    v_s = [_stack(x, first) for x in v]

    prod = _each(lambda a_, r_, b_, k_: _dg(jnp.concatenate([a_, r_], axis=0),
                                            jnp.concatenate([b_, k_], axis=0), _NT), a_t, r_t, b_t, k_t)
    l_ab = [jnp.where(strict, x[:n, :n], 0.0) for x in prod]
    l_ak = [jnp.where(strict, x[:n, n:], 0.0) for x in prod]
    m_r = [jnp.where(jnp.concatenate([incl, incl], axis=1), x[n:], 0.0) for x in prod]

    solve = _unit_lower_inverses(l_ab, c)
    lv = _each(lambda l_, v_: _dg(l_, v_, _NN), l_ak, v_s)
    au = _each(lambda t_, a_, x: _dg(t_, jnp.concatenate([a_, x], axis=1), _NN), solve, a_t, lv)
    rhs = _each(lambda x, v_: jnp.concatenate(
        [x, jnp.concatenate([jnp.zeros_like(v_), v_], axis=1)], axis=0), au, v_s)
    ry = _each(lambda m, x: _dg(m, x, _NN), m_r, rhs)
    gh = _each(lambda b_, k_, x: _dg(jnp.concatenate([b_, k_], axis=0), x, _TN), b_h, k_h, rhs)

    lhs = _each(lambda r_, x, y, e: jnp.concatenate(
        [r_ + x[:, :LANES], jnp.where(diag, jnp.exp(e), 0.0) + y[:, :LANES]], axis=0), r_t, ry, gh, cum_end)
    step = [_dg3(x, s_ref[p]) for p, x in enumerate(lhs)]
    for p in range(group):
        s_ref[p] = step[p][n:] + gh[p][:, LANES:]
    y = _each(lambda st, x: st[:c] + st[c:n] + x[:c, LANES:] + x[c:, LANES:], step, ry)

    mean = [_sum_sel(x, pair_mean, 2) for x in y]
    d = _each(lambda x, m: x - m, y, mean)
    var = [_sum_sel(x * x, pair_mean, 2) for x in d]
    bonus = _each(lambda r_, k_, w: _sum_sel(r_ * k_ * w, pair_sum, 2), r, k_mod, par(rk_ref))
    for p, sl in enumerate(lanes):
        yn = d[p] * lax.rsqrt(var[p] + GN_EPS) * lnw_ref[:, sl] + lnb_ref[:, sl]
        o_ref[0, :, sl] = ((yn + bonus[p] * v[p]) * g[p]).astype(o_ref.dtype)


def _rwkv_mix(p, pw, shift0, mu, w0, w2, a0, a2, g2, k_k, k_a, r_k, ln_w, ln_b, s0, t_valid):
    b, t_in, _ = p.shape
    d_rw = w0.shape[1]
    pairs = d_rw // LANES
    c = SCAN_CHUNK
    t = -(-t_in // c) * c
    if t != t_in:
        p = jnp.pad(p, ((0, 0), (0, t - t_in), (0, 0)))
    full = lambda a: pl.BlockSpec(a.shape, lambda bi, ci: (0,) * a.ndim)
    st_spec = pl.BlockSpec((1, pairs, LANES, LANES), lambda bi, ci: (bi, 0, 0, 0))
    params = (mu, w0, w2, a0, a2, g2, k_k, k_a, r_k, ln_w, ln_b)
    o, s_t = pl.pallas_call(
        functools.partial(_rwkv_mix_kernel, c=c, t_valid=t_valid, d_rw=d_rw, wd=w2.shape[0],
                          wa=a2.shape[0]),
        grid=(b, t // c),
        in_specs=[pl.BlockSpec((1, c, pw), lambda bi, ci: (bi, ci, 0)),
                  pl.BlockSpec((1, 1, pw), lambda bi, ci: (bi, 0, 0))]
                 + [full(a) for a in params] + [st_spec],
        out_specs=[pl.BlockSpec((1, c, d_rw), lambda bi, ci: (bi, ci, 0)), st_spec],
        out_shape=[jax.ShapeDtypeStruct((b, t, d_rw), BF16),
                   jax.ShapeDtypeStruct((b, pairs, LANES, LANES), F32)],
        scratch_shapes=[pltpu.VMEM((pairs, LANES, LANES), F32), pltpu.VMEM((1, pw), F32)],
        compiler_params=_params("parallel", "arbitrary"),
        name="rwkv_mix",
    )(p, shift0, *params, s0)
    return o[:, :t_in], s_t


def _state_to_pairs(s):
    b, h, e, _ = s.shape
    st = jnp.swapaxes(s, -1, -2).reshape(b, h // 2, 2, e, e)
    z = jnp.zeros_like(st[:, :, 0])
    top = jnp.concatenate([st[:, :, 0], z], axis=-1)
    bot = jnp.concatenate([z, st[:, :, 1]], axis=-1)
    return jnp.concatenate([top, bot], axis=-2)


def _pairs_to_state(sp):
    b, pairs = sp.shape[:2]
    e = HEAD_DIM
    blocks = jnp.stack([sp[:, :, :e, :e], sp[:, :, e:, e:]], axis=2)
    return jnp.swapaxes(blocks, -1, -2).reshape(b, 2 * pairs, e, e)


def _pad_cols(x, width):
    return jnp.pad(x, [(0, 0)] * (x.ndim - 1) + [(0, width - x.shape[-1])])


def _rw_layout(x, d_rw, ranks, widths):
    parts = [x[..., :3 * d_rw]]
    o = 3 * d_rw
    for rank, width in zip(ranks, widths):
        parts.append(_pad_cols(x[..., o:o + rank], width))
        o += rank
    return jnp.concatenate(parts, axis=-1)


def _rw_unlayout(x, d_rw, ranks, widths):
    parts = [x[..., :3 * d_rw]]
    o = 3 * d_rw
    for rank, width in zip(ranks, widths):
        parts.append(x[..., o:o + rank])
        o += width
    return jnp.concatenate(parts, axis=-1)


def _layer(h, lw, cache, s0_pairs, shift0, t_valid, tri):
    b, t, d = h.shape
    d_sb = lw["sb_gain"].shape[-1]
    x = h.reshape(b * t, d)
    x = _ffn(x, lw["ffn1_pre"], lw["ffn1_post"], lw["ffn1_gate"], lw["ffn1_up"], lw["ffn1_down"])
    p = _norm_matmul(x, lw["mix_pre"], lw["w_in"], MIX_IN_COLS).reshape(b, t, -1)
    rw_cols = p.shape[-1] - 3 * d_sb
    if cache is None:
        o_sb = _sb_prompt(p, rw_cols, d_sb, lw["sb_gain"], tri)
    else:
        o_sb = _sb_sample(p[:, :, rw_cols:], cache[0], cache[1], cache[2], lw["sb_gain"], tri)
    o_rw, s_t = _rwkv_mix(p, rw_cols, shift0, lw["mu"], lw["w0"], lw["w2"], lw["a0"], lw["a2"], lw["g2"],
                          lw["k_k"], lw["k_a"], lw["r_k"], lw["ln_w"], lw["ln_b"], s0_pairs, t_valid)
    x = _mix_out(o_sb.reshape(b * t, -1), o_rw.reshape(b * t, -1), lw["w_out_sb"], lw["w_out_rw"],
                 x, lw["mix_post"])
    x = _ffn(x, lw["ffn2_pre"], lw["ffn2_post"], lw["ffn2_gate"], lw["ffn2_up"], lw["ffn2_down"])
    k_new = p[:, :t_valid, rw_cols + d_sb:rw_cols + 2 * d_sb]
    v_new = p[:, :t_valid, rw_cols + 2 * d_sb:]
    return x.reshape(b, t, d), k_new, v_new, s_t, p[:, t_valid - 1, :rw_cols]


def kernel(x_prompt, x_sample, cache_sb_k, cache_sb_v, state_rwkv_S, state_rwkv_shift, meta_tokens, ffn1_norm_pre, ffn1_norm_post, ffn1_w_gate, ffn1_w_up, ffn1_w_down, mix_norm_pre, mix_norm_post, w_in, sb_out_gain, rwkv_mu, rwkv_w0, rwkv_w2, rwkv_a0, rwkv_a2, rwkv_g2, rwkv_k_k, rwkv_k_a, rwkv_r_k, rwkv_ln_w, rwkv_ln_b, w_out, ffn2_norm_pre, ffn2_norm_post, ffn2_w_gate, ffn2_w_up, ffn2_w_down):
    depth, d = ffn1_norm_pre.shape
    bp, seq, _ = x_prompt.shape
    bs, dec_seq, _ = x_sample.shape
    n_meta = meta_tokens.shape[0]
    h_sb = sb_out_gain.shape[1]
    d_sb = h_sb * HEAD_DIM
    d_rw = rwkv_w0.shape[1]
    h_rw = d_rw // HEAD_DIM
    ranks = (rwkv_w2.shape[1], rwkv_a2.shape[1], rwkv_g2.shape[1])
    widths = tuple(-(-r // LANES) * LANES for r in ranks)
    p_rw_cols = 3 * d_rw + sum(ranks)
    past = cache_sb_k.shape[2]

    t_valid = n_meta + seq
    t_pad = -(-t_valid // ATTN_BLOCK) * ATTN_BLOCK
    meta = jnp.broadcast_to(meta_tokens[None].astype(x_prompt.dtype), (bp, n_meta, d))
    hp = jnp.concatenate([meta, x_prompt, jnp.zeros((bp, t_pad - t_valid, d), x_prompt.dtype)], axis=1)
    hs = x_sample

    tri = _strict_upper(ATTN_BLOCK)
    feature_major = lambda c: jnp.transpose(c, (0, 1, 3, 4, 2)).reshape(depth, bs, d_sb, past)
    cache_kt, cache_vt = feature_major(cache_sb_k), feature_major(cache_sb_v)
    row = lambda x: x.reshape(1, -1)
    pad_rows = lambda w, width: jnp.pad(w, ((0, width - w.shape[0]), (0, 0)))
    w_in_rw = _rw_layout(w_in[:, :, 3 * d_sb:], d_rw, ranks, widths)
    w_in_all = jnp.concatenate([w_in_rw, w_in[:, :, :3 * d_sb]], axis=-1).astype(BF16)
    ffn_w = [_to_bf16(w) for w in (ffn1_w_gate, ffn1_w_up, ffn1_w_down, ffn2_w_gate, ffn2_w_up, ffn2_w_down)]
    mu_all = _rw_layout(rwkv_mu, d_rw, ranks, widths)

    outs = [[] for _ in range(8)]
    s0_p = jnp.zeros((bp, h_rw // 2, LANES, LANES), F32)
    shift0_p = jnp.zeros((bp, 1, w_in_rw.shape[-1]), F32)
    for l in range(depth):
        lw = dict(
            ffn1_pre=row(ffn1_norm_pre[l]), ffn1_post=row(ffn1_norm_post[l]),
            ffn1_gate=ffn_w[0][l], ffn1_up=ffn_w[1][l], ffn1_down=ffn_w[2][l],
            ffn2_pre=row(ffn2_norm_pre[l]), ffn2_post=row(ffn2_norm_post[l]),
            ffn2_gate=ffn_w[3][l], ffn2_up=ffn_w[4][l], ffn2_down=ffn_w[5][l],
            mix_pre=row(mix_norm_pre[l]), mix_post=row(mix_norm_post[l]),
            w_in=w_in_all[l], sb_gain=row(sb_out_gain[l]),
            mu=row(mu_all[l]), w0=row(rwkv_w0[l]), a0=row(rwkv_a0[l]),
            w2=pad_rows(rwkv_w2[l], widths[0]).astype(BF16),
            a2=pad_rows(rwkv_a2[l], widths[1]).astype(BF16),
            g2=pad_rows(rwkv_g2[l], widths[2]).astype(BF16),
            k_k=row(rwkv_k_k[l]), k_a=row(rwkv_k_a[l]), r_k=row(rwkv_r_k[l]),
            ln_w=row(rwkv_ln_w[l]), ln_b=row(rwkv_ln_b[l]),
            w_out_sb=w_out[l, :d_sb].astype(BF16), w_out_rw=w_out[l, d_sb:].astype(BF16),
        )
        hp, kp, vp, sp, shp = _layer(hp, lw, None, s0_p, shift0_p, t_valid, tri)
        cache = (cache_kt, cache_vt, l)
        shift0_s = _rw_layout(state_rwkv_shift[l], d_rw, ranks, widths)[:, None, :]
        hs, ks, vs, ss, shs = _layer(hs, lw, cache, _state_to_pairs(state_rwkv_S[l].astype(F32)),
                                     shift0_s, dec_seq, tri)
        per_layer = (
            kp.reshape(bp, t_valid, h_sb, HEAD_DIM), vp.reshape(bp, t_valid, h_sb, HEAD_DIM),
            _pairs_to_state(sp).astype(state_rwkv_S.dtype),
            _rw_unlayout(shp, d_rw, ranks, widths).astype(state_rwkv_shift.dtype),
            ks.reshape(bs, dec_seq, h_sb, HEAD_DIM), vs.reshape(bs, dec_seq, h_sb, HEAD_DIM),
            _pairs_to_state(ss).astype(state_rwkv_S.dtype),
            _rw_unlayout(shs, d_rw, ranks, widths).astype(state_rwkv_shift.dtype),
        )
        for acc, val in zip(outs, per_layer):
            acc.append(val)
    assert p_rw_cols == state_rwkv_shift.shape[-1]
    y_prompt = hp[:, n_meta:t_valid]
    return (y_prompt, hs) + tuple(jnp.stack(o) for o in outs)
```
